```python
import jax, jax.numpy as jnp
from jax import lax
import numpy as np

D_MODEL = 2048
BATCH = 4
SEQ = 4096
DEPTH = 4

GRID_W = 64
CTX_LEN = 256
W_A = 1024
LRU_HEADS = 8
LRU_HEAD_DIM = W_A // LRU_HEADS
LRU_C = 8.0
CONV_A_WIDTH = 4
CONV_A_PAD_LEFT = 2
W_B = 1024
POOL_WINDOWS = (2, 4, 8, 16)
N_POOL_GROUPS = len(POOL_WINDOWS)
POOL_GROUP = W_B // N_POOL_GROUPS
W_C = 1024
CONV_C_WIDTH = 31
CONV_C_PAD_LEFT = (CONV_C_WIDTH - 1) // 2
N_BRANCH = 3
BRANCH_W = 1024
EPS = 1e-6
IN_SPLITS = (W_A, W_A, W_B, W_B, W_C, W_C, W_C, N_BRANCH * D_MODEL)
P_IN = sum(IN_SPLITS)

kernel_name = "hybrid_rglru_pool_conformer_dit"


def _rmsnorm(x, g):
    xf = x.astype(jnp.float32)
    y = xf * lax.rsqrt(jnp.mean(xf * xf, axis=-1, keepdims=True) + EPS)
    return (y * g.astype(jnp.float32)).astype(x.dtype)


def _layernorm(x, g, b):
    xf = x.astype(jnp.float32)
    mu = jnp.mean(xf, axis=-1, keepdims=True)
    var = jnp.mean(jnp.square(xf - mu), axis=-1, keepdims=True)
    y = (xf - mu) * lax.rsqrt(var + EPS)
    return (y * g.astype(jnp.float32) + b.astype(jnp.float32)).astype(x.dtype)


def _split_cols(p):
    idx = [int(v) for v in np.cumsum(IN_SPLITS)[:-1]]
    return jnp.split(p, idx, axis=-1)


def _dwconv(x, w, b, pad_left):
    k = w.shape[0]
    y = lax.conv_general_dilated(
        x, w[:, None, :].astype(x.dtype), window_strides=(1,),
        padding=[(pad_left, k - 1 - pad_left)],
        dimension_numbers=("NWC", "WIO", "NWC"),
        feature_group_count=x.shape[-1])
    return y + b.astype(x.dtype)


def _block_diag(u, w, b):
    bs, L, _ = u.shape
    uh = u.reshape(bs, L, LRU_HEADS, LRU_HEAD_DIM)
    y = jnp.einsum("blhi,hij->blhj", uh, w.astype(u.dtype)) + b.astype(u.dtype)
    return y.reshape(bs, L, W_A)


def _lru_coeffs(u, wr, br, wi, bi, lam):
    r = jax.nn.sigmoid(_block_diag(u, wr, br).astype(jnp.float32))
    i = jax.nn.sigmoid(_block_diag(u, wi, bi).astype(jnp.float32))
    log_a = -LRU_C * r * jax.nn.softplus(-lam.astype(jnp.float32))
    a = jnp.exp(log_a)
    mult = jnp.sqrt(-jnp.expm1(2.0 * log_a))
    return a, mult * i * u.astype(jnp.float32)


def _linear_scan(a, b, h0=None):
    if h0 is not None:
        b = b.at[:, 0].add(a[:, 0] * h0)

    def combine(e1, e2):
        a1, b1 = e1
        a2, b2 = e2
        return a1 * a2, a2 * b1 + b2

    _, h = lax.associative_scan(combine, (a, b), axis=1)
    return h


def _rglru_bidir(u_ctx, u_lat, wr, br, wi, bi, lam, need_ctx):
    y_lat = None
    y_ctx = None
    for d in range(2):
        uc = u_ctx if d == 0 else jnp.flip(u_ctx, axis=1)
        ul = u_lat if d == 0 else jnp.flip(u_lat, axis=1)
        a_c, b_c = _lru_coeffs(uc, wr[d], br[d], wi[d], bi[d], lam[d])
        h_c = _linear_scan(a_c, b_c)
        a_l, b_l = _lru_coeffs(ul, wr[d], br[d], wi[d], bi[d], lam[d])
        h_l = _linear_scan(a_l, b_l, h_c[:, -1])
        if d == 1:
            h_l = jnp.flip(h_l, axis=1)
            h_c = jnp.flip(h_c, axis=1)
        y_lat = h_l if y_lat is None else y_lat + h_l
        if need_ctx:
            y_ctx = h_c if y_ctx is None else y_ctx + h_c
    y_lat = y_lat.astype(u_lat.dtype)
    if need_ctx:
        y_ctx = y_ctx.astype(u_ctx.dtype)
    return y_ctx, y_lat


def _multiscale_pool(u, n_rows, w_pool, s_pool):
    bs, L, _ = u.shape
    n = L // n_rows
    uf = u.astype(jnp.float32).reshape(bs, n_rows, n, W_B)
    s = jnp.cumsum(uf, axis=2)
    s = jnp.concatenate([jnp.zeros_like(s[:, :, :1]), s], axis=2)
    t = np.arange(n)
    parts = []
    for k, w in enumerate(POOL_WINDOWS):
        lo = np.maximum(t - w // 2, 0)
        hi = np.minimum(t + w // 2, n)
        cnt = (hi - lo).astype(np.float32)[:, None]
        sk = s[..., k * POOL_GROUP:(k + 1) * POOL_GROUP]
        mean = (jnp.take(sk, hi, axis=2) - jnp.take(sk, lo, axis=2)) / cnt
        parts.append(mean - uf[..., k * POOL_GROUP:(k + 1) * POOL_GROUP])
    p = jnp.stack(parts, axis=-2).astype(u.dtype)
    y = jnp.einsum("brngi,gij->brngj", p, w_pool.astype(u.dtype))
    return y.reshape(bs, L, W_B) * s_pool.astype(u.dtype)


def _conformer_conv(v, g, w_dw, b_dw, ln_g, ln_b, w_pw, b_pw):
    y = v * jax.nn.sigmoid(g)
    y = _dwconv(y, w_dw, b_dw, CONV_C_PAD_LEFT)
    y = jax.nn.silu(_layernorm(y, ln_g, ln_b))
    return y @ w_pw.astype(y.dtype) + b_pw.astype(y.dtype)


def _merge(ys, zs, gl, w_bout, w_out, b_out):
    gates = jax.nn.sigmoid(gl.reshape(gl.shape[:-1] + (N_BRANCH, D_MODEL)))
    acc = None
    for k in range(N_BRANCH):
        t = gates[..., k, :] * ((ys[k] * jax.nn.silu(zs[k])) @ w_bout[k].astype(ys[k].dtype))
        acc = t if acc is None else acc + t
    return acc @ w_out.astype(acc.dtype) + b_out.astype(acc.dtype)


def _stream_out(ya, cols, n_rows, pool_w, pool_scale, convc_w, convc_b, lnc_g, lnc_b, pwc_w, pwc_b,
                w_bout, w_out, b_out):
    _, za, xb, zb, cv, cg, zc, gl = cols
    yb = _multiscale_pool(xb, n_rows, pool_w, pool_scale)
    yc = _conformer_conv(cv, cg, convc_w, convc_b, lnc_g, lnc_b, pwc_w, pwc_b)
    return _merge((ya, yb, yc), (za, zb, zc), gl, w_bout, w_out, b_out)


def setup_inputs(seed: int = 0) -> dict:
    key = jax.random.key(seed)
    ks = jax.random.split(key, 32)
    f32 = jnp.float32

    def nrm(k, shape, scale):
        return jax.random.normal(k, shape, f32) * scale

    u = jax.random.uniform(ks[14], (DEPTH, 2, W_A), f32, minval=0.9, maxval=0.999)
    s = u ** (1.0 / LRU_C)
    lru_lambda = jnp.log(s) - jnp.log1p(-s)
    return {
        "x": nrm(ks[0], (BATCH, SEQ, D_MODEL), 1.0),
        "c": nrm(ks[1], (BATCH, D_MODEL), 1.0),
        "ctx": nrm(ks[2], (BATCH, CTX_LEN, D_MODEL), 1.0),
        "c_ctx": nrm(ks[3], (D_MODEL,), 1.0),
        "norm_g": 1.0 + nrm(ks[4], (DEPTH, D_MODEL), 0.05),
        "w_ada": nrm(ks[5], (DEPTH, D_MODEL, 3 * D_MODEL), 0.5 * D_MODEL ** -0.5),
        "b_ada": nrm(ks[6], (DEPTH, 3 * D_MODEL), 0.02),
        "w_in": nrm(ks[7], (DEPTH, D_MODEL, P_IN), D_MODEL ** -0.5),
        "conv_a_w": nrm(ks[8], (DEPTH, CONV_A_WIDTH, W_A), CONV_A_WIDTH ** -0.5),
        "conv_a_b": nrm(ks[9], (DEPTH, W_A), 0.02),
        "lru_wr": nrm(ks[10], (DEPTH, 2, LRU_HEADS, LRU_HEAD_DIM, LRU_HEAD_DIM), LRU_HEAD_DIM ** -0.5),
        "lru_br": nrm(ks[11], (DEPTH, 2, LRU_HEADS, LRU_HEAD_DIM), 0.02),
        "lru_wi": nrm(ks[12], (DEPTH, 2, LRU_HEADS, LRU_HEAD_DIM, LRU_HEAD_DIM), LRU_HEAD_DIM ** -0.5),
        "lru_bi": nrm(ks[13], (DEPTH, 2, LRU_HEADS, LRU_HEAD_DIM), 0.02),
        "lru_lambda": lru_lambda,
        "pool_w": nrm(ks[15], (DEPTH, N_POOL_GROUPS, POOL_GROUP, POOL_GROUP), POOL_GROUP ** -0.5),
        "pool_scale": 1.0 + nrm(ks[16], (DEPTH, W_B), 0.1),
        "convc_w": nrm(ks[17], (DEPTH, CONV_C_WIDTH, W_C), CONV_C_WIDTH ** -0.5),
        "convc_b": nrm(ks[18], (DEPTH, W_C), 0.02),
        "lnc_g": 1.0 + nrm(ks[19], (DEPTH, W_C), 0.05),
        "lnc_b": nrm(ks[20], (DEPTH, W_C), 0.02),
        "pwc_w": nrm(ks[21], (DEPTH, W_C, W_C), W_C ** -0.5),
        "pwc_b": nrm(ks[22], (DEPTH, W_C), 0.02),
        "w_bout": nrm(ks[23], (DEPTH, N_BRANCH, BRANCH_W, D_MODEL), BRANCH_W ** -0.5),
        "w_out": nrm(ks[24], (DEPTH, D_MODEL, D_MODEL), D_MODEL ** -0.5),
        "b_out": nrm(ks[25], (DEPTH, D_MODEL), 0.02),
        "final_g": 1.0 + nrm(ks[26], (D_MODEL,), 0.05),
    }


def reference(x, c, ctx, c_ctx, norm_g, w_ada, b_ada, w_in, conv_a_w, conv_a_b, lru_wr, lru_br, lru_wi,
              lru_bi, lru_lambda, pool_w, pool_scale, convc_w, convc_b, lnc_g, lnc_b, pwc_w, pwc_b,
              w_bout, w_out, b_out, final_g):
    L = x.shape[1]
    rows = L // GRID_W
    xc = ctx
    for l in range(DEPTH):
        last = l == DEPTH - 1
        mod = jax.nn.silu(c) @ w_ada[l].astype(c.dtype) + b_ada[l].astype(c.dtype)
        shift, scale, gate = jnp.split(mod, 3, axis=-1)
        mod_c = jax.nn.silu(c_ctx) @ w_ada[l].astype(c_ctx.dtype) + b_ada[l].astype(c_ctx.dtype)
        shift_c, scale_c, gate_c = jnp.split(mod_c, 3, axis=-1)

        h = _rmsnorm(x, norm_g[l]) * (1.0 + scale[:, None, :]) + shift[:, None, :]
        hc = _rmsnorm(xc, norm_g[l]) * (1.0 + scale_c) + shift_c

        cols = _split_cols(h @ w_in[l].astype(h.dtype))
        if last:
            xa_c = hc @ w_in[l][:, :W_A].astype(hc.dtype)
            cols_c = None
        else:
            cols_c = _split_cols(hc @ w_in[l].astype(hc.dtype))
            xa_c = cols_c[0]

        ua = _dwconv(cols[0], conv_a_w[l], conv_a_b[l], CONV_A_PAD_LEFT)
        ua_c = _dwconv(xa_c, conv_a_w[l], conv_a_b[l], CONV_A_PAD_LEFT)
        ya_c, ya = _rglru_bidir(ua_c, ua, lru_wr[l], lru_br[l], lru_wi[l], lru_bi[l], lru_lambda[l],
                                need_ctx=not last)

        out = _stream_out(ya, cols, rows, pool_w[l], pool_scale[l], convc_w[l], convc_b[l], lnc_g[l],
                          lnc_b[l], pwc_w[l], pwc_b[l], w_bout[l], w_out[l], b_out[l])
        x = x + gate[:, None, :] * out
        if not last:
            out_c = _stream_out(ya_c, cols_c, 1, pool_w[l], pool_scale[l], convc_w[l], convc_b[l], lnc_g[l],
                                lnc_b[l], pwc_w[l], pwc_b[l], w_bout[l], w_out[l], b_out[l])
            xc = xc + gate_c * out_c
    return _rmsnorm(x, final_g)
```

```python
import functools

import jax
import jax.numpy as jnp
from jax import lax
from jax.experimental import pallas as pl
from jax.experimental.pallas import tpu as pltpu

F32 = jnp.float32
BF16 = jnp.bfloat16

D_MODEL = 2048
BRANCH_W = 1024
LRU_HEADS = 8
HEAD_DIM = BRANCH_W // LRU_HEADS
LRU_C = 8.0
CONV_A_WIDTH = 4
CONV_A_PAD_LEFT = 2
GRID_W = 64
POOL_WINDOWS = (2, 4, 8, 16)
POOL_GROUP = BRANCH_W // len(POOL_WINDOWS)
CONV_C_WIDTH = 31
CONV_C_PAD_LEFT = (CONV_C_WIDTH - 1) // 2
N_BRANCH = 3
EPS = 1e-6
P_IN = 7 * BRANCH_W + N_BRANCH * D_MODEL
COL_XA, COL_ZA, COL_XB, COL_ZB, COL_CV, COL_CG, COL_ZC, COL_GL = (k * BRANCH_W for k in range(8))

SUBLANES = 8
V7X_VMEM_LIMIT_BYTES = 56 * 1024 * 1024


def _params(*sem):
    return pltpu.CompilerParams(dimension_semantics=sem, vmem_limit_bytes=V7X_VMEM_LIMIT_BYTES)


def _silu(v):
    return v * jax.nn.sigmoid(v)


def _norm_mod(x, g, scale, shift):
    y = x * lax.rsqrt(jnp.mean(x * x, axis=-1, keepdims=True) + EPS) * g
    return y * (1.0 + scale) + shift


def _ada_kernel(c_ref, w_ref, b_ref, o_ref):
    s = _silu(c_ref[...])
    o_ref[...] = jnp.dot(s.astype(BF16), w_ref[...].astype(BF16), preferred_element_type=F32) + b_ref[...]


def _ada(cc, w_ada, b_ada, tn=512):
    depth, d, n = w_ada.shape
    rows = cc.shape[0]
    return pl.pallas_call(
        _ada_kernel,
        out_shape=jax.ShapeDtypeStruct((depth, rows, n), F32),
        grid=(depth, n // tn),
        in_specs=[
            pl.BlockSpec((rows, d), lambda l, j: (0, 0)),
            pl.BlockSpec((None, d, tn), lambda l, j: (l, 0, j)),
            pl.BlockSpec((None, 1, tn), lambda l, j: (l, 0, j)),
        ],
        out_specs=pl.BlockSpec((None, rows, tn), lambda l, j: (l, 0, j)),
        compiler_params=_params("parallel", "parallel"),
        name="ada_mod",
    )(cc, w_ada, b_ada.reshape(depth, 1, n))


def _norm_kernel(x_ref, g_ref, sc_ref, sh_ref, h_ref):
    h_ref[...] = _norm_mod(x_ref[...], g_ref[...], sc_ref[...], sh_ref[...]).astype(BF16)


def _norm(x3, g, scale, shift, tm):
    G, T, d = x3.shape
    mod_spec = pl.BlockSpec((None, 1, d), lambda gi, i: (gi, 0, 0))
    return pl.pallas_call(
        _norm_kernel,
        out_shape=jax.ShapeDtypeStruct((G, T, d), BF16),
        grid=(G, T // tm),
        in_specs=[
            pl.BlockSpec((None, tm, d), lambda gi, i: (gi, i, 0)),
            pl.BlockSpec((1, d), lambda gi, i: (0, 0)),
            mod_spec, mod_spec,
        ],
        out_specs=pl.BlockSpec((None, tm, d), lambda gi, i: (gi, i, 0)),
        compiler_params=_params("parallel", "parallel"),
        name="norm_mod",
    )(x3, g, scale, shift)


def _mm_kernel(h_ref, w_ref, o_ref):
    o_ref[...] = jnp.dot(h_ref[...], w_ref[...], preferred_element_type=F32)


def _in_proj(h, w, n_cols, tm, tn):
    T, d = h.shape
    return pl.pallas_call(
        _mm_kernel,
        out_shape=jax.ShapeDtypeStruct((T, n_cols), F32),
        grid=(T // tm, n_cols // tn),
        in_specs=[
            pl.BlockSpec((tm, d), lambda i, j: (i, 0)),
            pl.BlockSpec((d, tn), lambda i, j: (0, j)),
        ],
        out_specs=pl.BlockSpec((tm, tn), lambda i, j: (i, j)),
        compiler_params=_params("parallel", "parallel"),
        name="in_proj",
    )(h, w)


def _rglru_kernel(xa_ref, za_ref, cw_ref, cb_ref, wri_ref, bri_ref, lam_ref, h0_ref,
                  ga_ref, hfin_ref, ua_s, a0_s, b0_s, a1_s, b1_s, hf_s, hb_s, *, L, RB):
    xa = xa_ref[...]
    row = lax.broadcasted_iota(jnp.int32, (L, HEAD_DIM), 0)
    ua = jnp.zeros((L, HEAD_DIM), F32) + cb_ref[...]
    for k in range(CONV_A_WIDTH):
        o = k - CONV_A_PAD_LEFT
        sh = xa if o == 0 else pltpu.roll(xa, (-o) % L, axis=0)
        valid = (row + o >= 0) & (row + o < L)
        ua = ua + cw_ref[k:k + 1, :] * jnp.where(valid, sh, 0.0)
    ua_s[...] = ua

    nlam = -lam_ref[...]
    sp = jnp.maximum(nlam, 0.0) + jnp.log1p(jnp.exp(-jnp.abs(nlam)))

    def coeffs(i, _):
        r0 = pl.multiple_of(i * RB, RB)
        u = ua_s[pl.ds(r0, RB), :]
        g = jnp.dot(u.astype(BF16), wri_ref[...], preferred_element_type=F32) + bri_ref[...]
        for d, (a_s, b_s) in enumerate(((a0_s, b0_s), (a1_s, b1_s))):
            c0 = 2 * d * HEAD_DIM
            r = jax.nn.sigmoid(g[:, c0:c0 + HEAD_DIM])
            ig = jax.nn.sigmoid(g[:, c0 + HEAD_DIM:c0 + 2 * HEAD_DIM])
            log_a = -LRU_C * r * sp[d:d + 1, :]
            a = jnp.exp(log_a)
            mult = jnp.sqrt(-jnp.tanh(log_a) * (a * a + 1.0))
            a_s[pl.ds(r0, RB), :] = a
            b_s[pl.ds(r0, RB), :] = mult * ig * u
        return 0

    lax.fori_loop(0, L // RB, coeffs, 0)

    n = L // SUBLANES
    srow = lax.broadcasted_iota(jnp.int32, (SUBLANES, HEAD_DIM), 0)

    def scan_step(k, carry):
        cf, cb = carry
        rf = pl.multiple_of(k * SUBLANES, SUBLANES)
        rb = pl.multiple_of((n - 1 - k) * SUBLANES, SUBLANES)
        A = a0_s[pl.ds(rf, SUBLANES), :]
        Bv = b0_s[pl.ds(rf, SUBLANES), :]
        for s in (1, 2, 4):
            keep = srow >= s
            A_sh = jnp.where(keep, pltpu.roll(A, s, axis=0), 1.0)
            B_sh = jnp.where(keep, pltpu.roll(Bv, s, axis=0), 0.0)
            Bv = A * B_sh + Bv
            A = A * A_sh
        hf = A * cf + Bv
        hf_s[pl.ds(rf, SUBLANES), :] = hf
        A = a1_s[pl.ds(rb, SUBLANES), :]
        Bv = b1_s[pl.ds(rb, SUBLANES), :]
        for s in (1, 2, 4):
            keep = srow < SUBLANES - s
            A_sh = jnp.where(keep, pltpu.roll(A, SUBLANES - s, axis=0), 1.0)
            B_sh = jnp.where(keep, pltpu.roll(Bv, SUBLANES - s, axis=0), 0.0)
            Bv = A * B_sh + Bv
            A = A * A_sh
        hb = A * cb + Bv
        hb_s[pl.ds(rb, SUBLANES), :] = hb
        return hf[SUBLANES - 1:SUBLANES, :], hb[0:1, :]

    cf, cb = lax.fori_loop(0, n, scan_step, (h0_ref[0:1, :], h0_ref[1:2, :]))
    hfin_ref[0:1, :] = cf
    hfin_ref[1:2, :] = cb
    ga_ref[...] = ((hf_s[...] + hb_s[...]) * _silu(za_ref[...])).astype(BF16)


def _rglru(p3, conv_w, conv_b, wri, bri, lam, h0):
    B, L, _ = p3.shape
    RB = min(L, 512)
    col = lambda off: pl.BlockSpec((None, L, HEAD_DIM), lambda b, h, o=off // HEAD_DIM: (b, 0, o + h))
    scr = pltpu.VMEM((L, HEAD_DIM), F32)
    return pl.pallas_call(
        functools.partial(_rglru_kernel, L=L, RB=RB),
        out_shape=(jax.ShapeDtypeStruct((B, L, BRANCH_W), BF16),
                   jax.ShapeDtypeStruct((B, 2, BRANCH_W), F32)),
        grid=(B, LRU_HEADS),
        in_specs=[
            col(COL_XA), col(COL_ZA),
            pl.BlockSpec((CONV_A_WIDTH, HEAD_DIM), lambda b, h: (0, h)),
            pl.BlockSpec((1, HEAD_DIM), lambda b, h: (0, h)),
            pl.BlockSpec((None, HEAD_DIM, 4 * HEAD_DIM), lambda b, h: (h, 0, 0)),
            pl.BlockSpec((None, 1, 4 * HEAD_DIM), lambda b, h: (h, 0, 0)),
            pl.BlockSpec((2, HEAD_DIM), lambda b, h: (0, h)),
            pl.BlockSpec((None, 2, HEAD_DIM), lambda b, h: (b, 0, h)),
        ],
        out_specs=(pl.BlockSpec((None, L, HEAD_DIM), lambda b, h: (b, 0, h)),
                   pl.BlockSpec((None, 2, HEAD_DIM), lambda b, h: (b, 0, h))),
        scratch_shapes=[scr] * 7,
        compiler_params=_params("parallel", "parallel"),
        name="rglru",
    )(p3, p3, conv_w, conv_b, wri, bri, lam, h0)


_CONVC_HALO = 16


def _glu_conv_kernel(cv_ref, cg_ref, w_ref, b_ref, o_ref, s_ref, *, L, RB):
    LP = L + 2 * _CONVC_HALO
    s_ref[0, 0:_CONVC_HALO, :] = jnp.zeros((_CONVC_HALO, HEAD_DIM), F32)
    s_ref[0, L + _CONVC_HALO:LP, :] = jnp.zeros((_CONVC_HALO, HEAD_DIM), F32)
    s_ref[0, _CONVC_HALO:L + _CONVC_HALO, :] = cv_ref[...] * jax.nn.sigmoid(cg_ref[...])
    for r in range(1, SUBLANES):
        s_ref[r, 0:LP - SUBLANES, :] = s_ref[0, r:r + LP - SUBLANES, :]

    first = _CONVC_HALO - CONV_C_PAD_LEFT

    def blk(i, _):
        r0 = pl.multiple_of(i * RB, RB)
        acc = jnp.zeros((RB, HEAD_DIM), F32) + b_ref[...]
        for k in range(CONV_C_WIDTH):
            q, r = divmod(first + k, SUBLANES)
            start = pl.multiple_of(r0 + q * SUBLANES, SUBLANES)
            acc = acc + w_ref[k:k + 1, :] * s_ref[r, pl.ds(start, RB), :]
        o_ref[pl.ds(r0, RB), :] = acc
        return 0

    lax.fori_loop(0, L // RB, blk, 0)


def _glu_conv(p3, w, b):
    B, L, _ = p3.shape
    RB = min(L, 256)
    col = lambda off: pl.BlockSpec((None, L, HEAD_DIM), lambda bi, h, o=off // HEAD_DIM: (bi, 0, o + h))
    return pl.pallas_call(
        functools.partial(_glu_conv_kernel, L=L, RB=RB),
        out_shape=jax.ShapeDtypeStruct((B, L, BRANCH_W), F32),
        grid=(B, BRANCH_W // HEAD_DIM),
        in_specs=[
            col(COL_CV), col(COL_CG),
            pl.BlockSpec((CONV_C_WIDTH, HEAD_DIM), lambda bi, h: (0, h)),
            pl.BlockSpec((1, HEAD_DIM), lambda bi, h: (0, h)),
        ],
        out_specs=pl.BlockSpec((None, L, HEAD_DIM), lambda bi, h: (bi, 0, h)),
        scratch_shapes=[pltpu.VMEM((SUBLANES, L + 2 * _CONVC_HALO, HEAD_DIM), F32)],
        compiler_params=_params("parallel", "parallel"),
        name="glu_conv",
    )(p3, p3, w, b)


def _pool_tail_kernel(xb_ref, zb_ref, yc_ref, zc_ref, pw_ref, ps_ref, lng_ref, lnb_ref, pwc_ref, pwcb_ref,
                      gb_ref, gc_ref, *, tm, n_row):
    pos = lax.broadcasted_iota(jnp.int32, (tm, POOL_GROUP), 0) % n_row
    for k, w in enumerate(POOL_WINDOWS):
        c0 = k * POOL_GROUP
        u = xb_ref[:, c0:c0 + POOL_GROUP]
        tot = u
        for o in range(-(w // 2), w // 2):
            if o == 0:
                continue
            sh = pltpu.roll(u, (-o) % tm, axis=0)
            tot = tot + jnp.where((pos + o >= 0) & (pos + o < n_row), sh, 0.0)
        cnt = jnp.minimum(pos + w // 2, n_row) - jnp.maximum(pos - w // 2, 0)
        p = tot / cnt.astype(F32) - u
        y = jnp.dot(p.astype(BF16), pw_ref[k], preferred_element_type=F32) * ps_ref[:, c0:c0 + POOL_GROUP]
        gb_ref[:, c0:c0 + POOL_GROUP] = (y * _silu(zb_ref[:, c0:c0 + POOL_GROUP])).astype(BF16)

    yc = yc_ref[...]
    mu = jnp.mean(yc, axis=-1, keepdims=True)
    dev = yc - mu
    var = jnp.mean(dev * dev, axis=-1, keepdims=True)
    t = _silu(dev * lax.rsqrt(var + EPS) * lng_ref[...] + lnb_ref[...])
    y = jnp.dot(t.astype(BF16), pwc_ref[...], preferred_element_type=F32) + pwcb_ref[...]
    gc_ref[...] = (y * _silu(zc_ref[...])).astype(BF16)


def _pool_tail(p, yconv, pool_w, pool_scale, ln_g, ln_b, pwc_w, pwc_b, tm, n_row):
    T = p.shape[0]
    col = lambda off: pl.BlockSpec((tm, BRANCH_W), lambda i, o=off // BRANCH_W: (i, o))
    full = lambda shape: pl.BlockSpec(shape, lambda i: (0,) * len(shape))
    tile = pl.BlockSpec((tm, BRANCH_W), lambda i: (i, 0))
    return pl.pallas_call(
        functools.partial(_pool_tail_kernel, tm=tm, n_row=n_row),
        out_shape=(jax.ShapeDtypeStruct((T, BRANCH_W), BF16), jax.ShapeDtypeStruct((T, BRANCH_W), BF16)),
        grid=(T // tm,),
        in_specs=[
            col(COL_XB), col(COL_ZB), tile, col(COL_ZC),
            full(pool_w.shape), full((1, BRANCH_W)), full((1, BRANCH_W)), full((1, BRANCH_W)),
            full((BRANCH_W, BRANCH_W)), full((1, BRANCH_W)),
        ],
        out_specs=(tile, tile),
        compiler_params=_params("parallel"),
        name="pool_tail",
    )(p, p, yconv, p, pool_w, pool_scale, ln_g, ln_b, pwc_w, pwc_b)


def _merge_kernel(ga_ref, gb_ref, gc_ref, w_ref, gl0_ref, gl1_ref, gl2_ref, o_ref):
    acc = None
    for k, (g_ref, gl_ref) in enumerate(((ga_ref, gl0_ref), (gb_ref, gl1_ref), (gc_ref, gl2_ref))):
        t = jax.nn.sigmoid(gl_ref[...]) * jnp.dot(g_ref[...], w_ref[k], preferred_element_type=F32)
        acc = t if acc is None else acc + t
    o_ref[...] = acc.astype(BF16)


def _merge(ga, gb, gc, w_bout, p, tm, tn):
    T = ga.shape[0]
    g_spec = pl.BlockSpec((tm, BRANCH_W), lambda i, j: (i, 0))
    gl = lambda k: pl.BlockSpec((tm, tn), lambda i, j, o=(COL_GL + k * D_MODEL) // tn: (i, o + j))
    return pl.pallas_call(
        _merge_kernel,
        out_shape=jax.ShapeDtypeStruct((T, D_MODEL), BF16),
        grid=(T // tm, D_MODEL // tn),
        in_specs=[
            g_spec, g_spec, g_spec,
            pl.BlockSpec((N_BRANCH, BRANCH_W, tn), lambda i, j: (0, 0, j)),
            gl(0), gl(1), gl(2),
        ],
        out_specs=pl.BlockSpec((tm, tn), lambda i, j: (i, j)),
        compiler_params=_params("parallel", "parallel"),
        name="merge",
    )(ga, gb, gc, w_bout, p, p, p)


def _out_mid_kernel(acc_ref, x_ref, w_ref, b_ref, gate_ref, g_ref, sc_ref, sh_ref, xo_ref, h_ref):
    out = jnp.dot(acc_ref[...], w_ref[...], preferred_element_type=F32) + b_ref[...]
    xn = x_ref[...] + gate_ref[...] * out
    xo_ref[...] = xn
    h_ref[...] = _norm_mod(xn, g_ref[...], sc_ref[...], sh_ref[...]).astype(BF16)


def _out_last_kernel(acc_ref, x_ref, w_ref, b_ref, gate_ref, g_ref, o_ref):
    out = jnp.dot(acc_ref[...], w_ref[...], preferred_element_type=F32) + b_ref[...]
    xn = x_ref[...] + gate_ref[...] * out
    o_ref[...] = xn * lax.rsqrt(jnp.mean(xn * xn, axis=-1, keepdims=True) + EPS) * g_ref[...]


def _out_proj(acc3, x3, w_out, b_out, gate, g_next, scale_next, shift_next, tm, last):
    G, T, d = x3.shape
    tile = pl.BlockSpec((None, tm, d), lambda gi, i: (gi, i, 0))
    mod = pl.BlockSpec((None, 1, d), lambda gi, i: (gi, 0, 0))
    vec = pl.BlockSpec((1, d), lambda gi, i: (0, 0))
    wspec = pl.BlockSpec((d, d), lambda gi, i: (0, 0))
    if last:
        return pl.pallas_call(
            _out_last_kernel,
            out_shape=jax.ShapeDtypeStruct((G, T, d), F32),
            grid=(G, T // tm),
            in_specs=[tile, tile, wspec, vec, mod, vec],
            out_specs=tile,
            compiler_params=_params("parallel", "parallel"),
            name="out_last",
        )(acc3, x3, w_out, b_out, gate, g_next)
    return pl.pallas_call(
        _out_mid_kernel,
        out_shape=(jax.ShapeDtypeStruct((G, T, d), F32), jax.ShapeDtypeStruct((G, T, d), BF16)),
        grid=(G, T // tm),
        in_specs=[tile, tile, wspec, vec, mod, vec, mod, mod],
        out_specs=(tile, tile),
        compiler_params=_params("parallel", "parallel"),
        name="out_mid",
    )(acc3, x3, w_out, b_out, gate, g_next, scale_next, shift_next)


def _mix(h, lw, B, L, n_row, h0, n_cols, tm_in, tm_tail, tm_merge):
    T = B * L
    p = _in_proj(h, lw["w_in"][:, :n_cols] if n_cols != P_IN else lw["w_in"], n_cols, tm_in, 512)
    p3 = p.reshape(B, L, n_cols)
    ga, hfin = _rglru(p3, lw["conv_a_w"], lw["conv_a_b"], lw["wri"], lw["bri"], lw["lam"], h0)
    if n_cols != P_IN:
        return None, hfin
    yconv = _glu_conv(p3, lw["convc_w"], lw["convc_b"])
    gb, gc = _pool_tail(p, yconv.reshape(T, BRANCH_W), lw["pool_w"], lw["pool_scale"], lw["lnc_g"],
                        lw["lnc_b"], lw["pwc_w"], lw["pwc_b"], tm_tail, n_row)
    acc = _merge(ga.reshape(T, BRANCH_W), gb, gc, lw["w_bout"], p, tm_merge, 512)
    return acc, hfin


def kernel(x, c, ctx, c_ctx, norm_g, w_ada, b_ada, w_in, conv_a_w, conv_a_b, lru_wr, lru_br, lru_wi, lru_bi,
           lru_lambda, pool_w, pool_scale, convc_w, convc_b, lnc_g, lnc_b, pwc_w, pwc_b, w_bout, w_out, b_out,
           final_g):
    B, L, d = x.shape
    CL = ctx.shape[1]
    depth = w_in.shape[0]
    TC = B * CL

    rows = -(-(B + 1) // SUBLANES) * SUBLANES
    cc = jnp.zeros((rows, d), F32).at[:B].set(c).at[B].set(c_ctx)
    mod = _ada(cc, w_ada, b_ada)

    def mods(l):
        sh, sc, gt = (mod[l, :, k * d:(k + 1) * d] for k in range(3))
        lat = tuple(v[:B].reshape(B, 1, d) for v in (sh, sc, gt))
        cx = tuple(v[B:B + 1].reshape(1, 1, d) for v in (sh, sc, gt))
        return lat, cx

    def layer_weights(l):
        wri = jnp.concatenate([lru_wr[l, 0], lru_wi[l, 0], lru_wr[l, 1], lru_wi[l, 1]], axis=-1)
        bri = jnp.concatenate([lru_br[l, 0], lru_bi[l, 0], lru_br[l, 1], lru_bi[l, 1]], axis=-1)
        return {
            "w_in": w_in[l].astype(BF16),
            "conv_a_w": conv_a_w[l], "conv_a_b": conv_a_b[l].reshape(1, -1),
            "wri": wri.astype(BF16), "bri": bri.reshape(LRU_HEADS, 1, 4 * HEAD_DIM),
            "lam": lru_lambda[l],
            "pool_w": pool_w[l].astype(BF16), "pool_scale": pool_scale[l].reshape(1, -1),
            "convc_w": convc_w[l], "convc_b": convc_b[l].reshape(1, -1),
            "lnc_g": lnc_g[l].reshape(1, -1), "lnc_b": lnc_b[l].reshape(1, -1),
            "pwc_w": pwc_w[l].astype(BF16), "pwc_b": pwc_b[l].reshape(1, -1),
            "w_bout": w_bout[l].astype(BF16),
            "w_out": w_out[l].astype(BF16), "b_out": b_out[l].reshape(1, -1),
        }

    tm_lat = min(L, 512)
    tm_ctx = min(TC, 512)
    x3 = x
    xc3 = ctx.reshape(1, TC, d)
    (sh, sc, _), (shc, scc, _) = mods(0)
    h = _norm(x3, norm_g[0].reshape(1, d), sc, sh, tm_lat)
    hc = _norm(xc3, norm_g[0].reshape(1, d), scc, shc, tm_ctx)
    zeros_h0 = jnp.zeros((B, 2, BRANCH_W), F32)

    for l in range(depth):
        last = l == depth - 1
        lw = layer_weights(l)
        (_, _, gate), (_, _, gate_c) = mods(l)
        acc_c, hfin = _mix(hc.reshape(TC, d), lw, B, CL, CL, zeros_h0, 2 * BRANCH_W if last else P_IN,
                           tm_ctx, CL, tm_ctx)
        acc, _ = _mix(h.reshape(B * L, d), lw, B, L, GRID_W, hfin, P_IN, min(B * L, 2048), tm_lat,
                      min(B * L, 1024))
        if last:
            return _out_proj(acc.reshape(B, L, d), x3, lw["w_out"], lw["b_out"], gate,
                             final_g.reshape(1, d), None, None, tm_lat, True)
        (sh, sc, _), (shc, scc, _) = mods(l + 1)
        g_next = norm_g[l + 1].reshape(1, d)
        x3, h = _out_proj(acc.reshape(B, L, d), x3, lw["w_out"], lw["b_out"], gate, g_next, sc, sh, tm_lat, False)
        xc3, hc = _out_proj(acc_c.reshape(1, TC, d), xc3, lw["w_out"], lw["b_out"], gate_c, g_next, scc, shc,
                            tm_ctx, False)
```

```python
import functools

import jax
import jax.numpy as jnp
from jax import lax
from jax.experimental import pallas as pl
from jax.experimental.pallas import tpu as pltpu

F32 = jnp.float32
BF16 = jnp.bfloat16

D_MODEL = 2048
BRANCH_W = 1024
LRU_HEADS = 8
HEAD_DIM = BRANCH_W // LRU_HEADS
LRU_C = 8.0
CONV_A_WIDTH = 4
CONV_A_PAD_LEFT = 2
GRID_W = 64
POOL_WINDOWS = (2, 4, 8, 16)
POOL_GROUP = BRANCH_W // len(POOL_WINDOWS)
CONV_C_WIDTH = 31
CONV_C_PAD_LEFT = (CONV_C_WIDTH - 1) // 2
N_BRANCH = 3
EPS = 1e-6
BLK_XA, BLK_ZA, BLK_XB, BLK_ZB, BLK_CV, BLK_CG, BLK_ZC, BLK_GL = range(8)
N_ACT_BLOCKS = 3 + N_BRANCH * D_MODEL // BRANCH_W

SUBLANES = 8
V7X_VMEM_LIMIT_BYTES = 56 * 1024 * 1024


def _params(*sem):
    return pltpu.CompilerParams(dimension_semantics=sem, vmem_limit_bytes=V7X_VMEM_LIMIT_BYTES)


def _silu(v):
    return v * jax.nn.sigmoid(v)


def _norm_mod(x, g, scale, shift):
    y = x * lax.rsqrt(jnp.mean(x * x, axis=-1, keepdims=True) + EPS) * g
    return y * (1.0 + scale) + shift


def _ada_kernel(c_ref, w_ref, b_ref, o_ref):
    s = _silu(c_ref[...])
    o_ref[...] = jnp.dot(s.astype(BF16), w_ref[...].astype(BF16), preferred_element_type=F32) + b_ref[...]


def _ada(cc, w_ada, b_ada, tn=512):
    depth, d, n = w_ada.shape
    rows = cc.shape[0]
    return pl.pallas_call(
        _ada_kernel,
        out_shape=jax.ShapeDtypeStruct((depth, rows, n), F32),
        grid=(depth, n // tn),
        in_specs=[
            pl.BlockSpec((rows, d), lambda l, j: (0, 0)),
            pl.BlockSpec((None, d, tn), lambda l, j: (l, 0, j)),
            pl.BlockSpec((None, 1, tn), lambda l, j: (l, 0, j)),
        ],
        out_specs=pl.BlockSpec((None, rows, tn), lambda l, j: (l, 0, j)),
        compiler_params=_params("parallel", "parallel"),
        name="ada_mod",
    )(cc, w_ada, b_ada.reshape(depth, 1, n))


def _norm_kernel(x_ref, g_ref, sc_ref, sh_ref, h_ref):
    h_ref[...] = _norm_mod(x_ref[...], g_ref[...], sc_ref[...], sh_ref[...]).astype(BF16)


def _norm(x3, g, scale, shift, tm):
    G, T, d = x3.shape
    mod_spec = pl.BlockSpec((None, 1, d), lambda gi, i: (gi, 0, 0))
    return pl.pallas_call(
        _norm_kernel,
        out_shape=jax.ShapeDtypeStruct((G, T, d), BF16),
        grid=(G, T // tm),
        in_specs=[
            pl.BlockSpec((None, tm, d), lambda gi, i: (gi, i, 0)),
            pl.BlockSpec((1, d), lambda gi, i: (0, 0)),
            mod_spec, mod_spec,
        ],
        out_specs=pl.BlockSpec((None, tm, d), lambda gi, i: (gi, i, 0)),
        compiler_params=_params("parallel", "parallel"),
        name="norm_mod",
    )(x3, g, scale, shift)


def _lin_kernel(h_ref, w_ref, o_ref):
    o_ref[...] = jnp.dot(h_ref[...], w_ref[...], preferred_element_type=F32)


def _glu_kernel(h_ref, wv_ref, wg_ref, o_ref):
    h = h_ref[...]
    v = jnp.dot(h, wv_ref[...], preferred_element_type=F32)
    g = jnp.dot(h, wg_ref[...], preferred_element_type=F32)
    o_ref[...] = v * jax.nn.sigmoid(g)


def _act_kernel(h_ref, w_ref, o_ref):
    acc = jnp.dot(h_ref[...], w_ref[...], preferred_element_type=F32)
    s = jax.nn.sigmoid(acc)
    o_ref[...] = jnp.where(pl.program_id(1) < 3, acc * s, s).astype(BF16)


def _w_spec(layer, d, blk_of_j):
    return pl.BlockSpec((None, d, BRANCH_W), lambda i, j: (layer, 0, blk_of_j(j)))


def _in_proj(kind, h, w_in, layer, n_blocks, tm):
    T, d = h.shape
    h_spec = pl.BlockSpec((tm, d), lambda i, j: (i, 0))
    if kind == "lin":
        body, dtype = _lin_kernel, F32
        w_specs = [_w_spec(layer, d, lambda j: (BLK_XB - BLK_XA) * j)]
    elif kind == "glu":
        body, dtype = _glu_kernel, F32
        w_specs = [_w_spec(layer, d, lambda j: BLK_CV + j), _w_spec(layer, d, lambda j: BLK_CG + j)]
    else:
        body, dtype = _act_kernel, BF16
        w_specs = [_w_spec(layer, d, lambda j: jnp.where(j < 3, 1 + 2 * j + j // 2, j + BLK_GL - 3))]
    return pl.pallas_call(
        body,
        out_shape=jax.ShapeDtypeStruct((T, n_blocks * BRANCH_W), dtype),
        grid=(T // tm, n_blocks),
        in_specs=[h_spec] + w_specs,
        out_specs=pl.BlockSpec((tm, BRANCH_W), lambda i, j: (i, j)),
        compiler_params=_params("parallel", "parallel"),
        name="in_proj_" + kind,
    )(h, *([w_in] * len(w_specs)))


def _rglru_kernel(xa_ref, cw_ref, cb_ref, wri_ref, bri_ref, lam_ref, h0_ref,
                  ya_ref, hfin_ref, ua_s, hf_s, af_s, hb_s, ab_s, *, L, JB):
    S = L // SUBLANES
    xa = xa_ref[...]
    row = lax.broadcasted_iota(jnp.int32, (L, HEAD_DIM), 0)
    ua = jnp.zeros((L, HEAD_DIM), F32) + cb_ref[...]
    for k in range(CONV_A_WIDTH):
        o = k - CONV_A_PAD_LEFT
        sh = xa if o == 0 else pltpu.roll(xa, (-o) % L, axis=0)
        valid = (row + o >= 0) & (row + o < L)
        ua = ua + cw_ref[k:k + 1, :] * jnp.where(valid, sh, 0.0)
    ua_s[...] = ua

    nlam = -lam_ref[...]
    sp = jnp.maximum(nlam, 0.0) + jnp.log1p(jnp.exp(-jnp.abs(nlam)))
    rate = -LRU_C * sp

    def seg_rows(j):
        return (pl.ds(j, SUBLANES, stride=S), slice(None))

    def coeffs(j0, d):
        u = jnp.concatenate([ua_s[seg_rows(j0 + v)] for v in range(JB)], axis=0)
        c0 = 2 * d * HEAD_DIM
        g = (jnp.dot(u.astype(BF16), wri_ref[:, c0:c0 + 2 * HEAD_DIM], preferred_element_type=F32)
             + bri_ref[:, c0:c0 + 2 * HEAD_DIM])
        r = jax.nn.sigmoid(g[:, :HEAD_DIM])
        ig = jax.nn.sigmoid(g[:, HEAD_DIM:])
        log_a = r * rate[d:d + 1, :]
        a = jnp.exp(log_a)
        mult = jnp.sqrt(-jnp.tanh(log_a) * (a * a + 1.0))
        return a, mult * ig * u

    def block(i, carry):
        hf, af, hb, ab = carry
        j0 = i * JB
        a, b = coeffs(j0, 0)
        for v in range(JB):
            av = a[v * SUBLANES:(v + 1) * SUBLANES]
            hf = av * hf + b[v * SUBLANES:(v + 1) * SUBLANES]
            af = av * af
            hf_s[j0 + v] = hf
            af_s[j0 + v] = af
        j0 = S - (i + 1) * JB
        a, b = coeffs(j0, 1)
        for v in reversed(range(JB)):
            av = a[v * SUBLANES:(v + 1) * SUBLANES]
            hb = av * hb + b[v * SUBLANES:(v + 1) * SUBLANES]
            ab = av * ab
            hb_s[j0 + v] = hb
            ab_s[j0 + v] = ab
        return hf, af, hb, ab

    zero = jnp.zeros((SUBLANES, HEAD_DIM), F32)
    one = jnp.ones((SUBLANES, HEAD_DIM), F32)
    hf, af, hb, ab = lax.fori_loop(0, S // JB, block, (zero, one, zero, one))

    cf = [h0_ref[0:1, :]]
    for s in range(SUBLANES):
        cf.append(hf[s:s + 1] + af[s:s + 1] * cf[-1])
    cb = [h0_ref[1:2, :]]
    for s in reversed(range(SUBLANES)):
        cb.append(hb[s:s + 1] + ab[s:s + 1] * cb[-1])
    cb = cb[::-1]
    hfin_ref[0:1, :] = cf[SUBLANES]
    hfin_ref[1:2, :] = cb[0]
    enter_f = jnp.concatenate(cf[:SUBLANES], axis=0)
    enter_b = jnp.concatenate(cb[1:], axis=0)

    def finish(i, _):
        for v in range(JB):
            j = i * JB + v
            ya_ref[seg_rows(j)] = (hf_s[j] + af_s[j] * enter_f) + (hb_s[j] + ab_s[j] * enter_b)
        return 0

    lax.fori_loop(0, S // JB, finish, 0)


def _rglru(lin3, conv_w, conv_b, wri, bri, lam, h0):
    B, L, _ = lin3.shape
    S = L // SUBLANES
    JB = min(S, 32)
    col = pl.BlockSpec((None, L, HEAD_DIM), lambda b, h: (b, 0, h))
    scr = pltpu.VMEM((S, SUBLANES, HEAD_DIM), F32)
    return pl.pallas_call(
        functools.partial(_rglru_kernel, L=L, JB=JB),
        out_shape=(jax.ShapeDtypeStruct((B, L, BRANCH_W), F32),
                   jax.ShapeDtypeStruct((B, 2, BRANCH_W), F32)),
        grid=(B, LRU_HEADS),
        in_specs=[
            col,
            pl.BlockSpec((CONV_A_WIDTH, HEAD_DIM), lambda b, h: (0, h)),
            pl.BlockSpec((1, HEAD_DIM), lambda b, h: (0, h)),
            pl.BlockSpec((None, HEAD_DIM, 4 * HEAD_DIM), lambda b, h: (h, 0, 0)),
            pl.BlockSpec((None, 1, 4 * HEAD_DIM), lambda b, h: (h, 0, 0)),
            pl.BlockSpec((2, HEAD_DIM), lambda b, h: (0, h)),
            pl.BlockSpec((None, 2, HEAD_DIM), lambda b, h: (b, 0, h)),
        ],
        out_specs=(col, pl.BlockSpec((None, 2, HEAD_DIM), lambda b, h: (b, 0, h))),
        scratch_shapes=[pltpu.VMEM((L, HEAD_DIM), F32)] + [scr] * 4,
        compiler_params=_params("parallel", "parallel"),
        name="rglru",
    )(lin3, conv_w, conv_b, wri, bri, lam, h0)


_CONVC_HALO = 16


def _dwconv_kernel(y_ref, w_ref, b_ref, o_ref, s_ref, *, L, RB):
    LP = L + 2 * _CONVC_HALO
    s_ref[0, 0:_CONVC_HALO, :] = jnp.zeros((_CONVC_HALO, HEAD_DIM), F32)
    s_ref[0, L + _CONVC_HALO:LP, :] = jnp.zeros((_CONVC_HALO, HEAD_DIM), F32)
    s_ref[0, _CONVC_HALO:L + _CONVC_HALO, :] = y_ref[...]
    for r in range(1, SUBLANES):
        s_ref[r, 0:LP - SUBLANES, :] = s_ref[0, r:r + LP - SUBLANES, :]

    first = _CONVC_HALO - CONV_C_PAD_LEFT

    def blk(i, _):
        r0 = pl.multiple_of(i * RB, RB)
        acc = jnp.zeros((RB, HEAD_DIM), F32) + b_ref[...]
        for k in range(CONV_C_WIDTH):
            q, r = divmod(first + k, SUBLANES)
            start = pl.multiple_of(r0 + q * SUBLANES, SUBLANES)
            acc = acc + w_ref[k:k + 1, :] * s_ref[r, pl.ds(start, RB), :]
        o_ref[pl.ds(r0, RB), :] = acc
        return 0

    lax.fori_loop(0, L // RB, blk, 0)


def _dwconv(y3, w, b):
    B, L, _ = y3.shape
    RB = min(L, 256)
    col = pl.BlockSpec((None, L, HEAD_DIM), lambda bi, h: (bi, 0, h))
    return pl.pallas_call(
        functools.partial(_dwconv_kernel, L=L, RB=RB),
        out_shape=jax.ShapeDtypeStruct((B, L, BRANCH_W), F32),
        grid=(B, BRANCH_W // HEAD_DIM),
        in_specs=[
            col,
            pl.BlockSpec((CONV_C_WIDTH, HEAD_DIM), lambda bi, h: (0, h)),
            pl.BlockSpec((1, HEAD_DIM), lambda bi, h: (0, h)),
        ],
        out_specs=col,
        scratch_shapes=[pltpu.VMEM((SUBLANES, L + 2 * _CONVC_HALO, HEAD_DIM), F32)],
        compiler_params=_params("parallel", "parallel"),
        name="dwconv",
    )(y3, w, b)


def _pool_tail_kernel(xb_ref, szb_ref, yc_ref, szc_ref, pw_ref, ps_ref, lng_ref, lnb_ref, pwc_ref, pwcb_ref,
                      gb_ref, gc_ref, *, tm, n_row):
    pos = lax.broadcasted_iota(jnp.int32, (tm, POOL_GROUP), 0) % n_row
    for k, w in enumerate(POOL_WINDOWS):
        c0 = k * POOL_GROUP
        u = xb_ref[:, c0:c0 + POOL_GROUP]
        tot = u
        for o in range(-(w // 2), w // 2):
            if o == 0:
                continue
            sh = pltpu.roll(u, (-o) % tm, axis=0)
            tot = tot + jnp.where((pos + o >= 0) & (pos + o < n_row), sh, 0.0)
        cnt = jnp.minimum(pos + w // 2, n_row) - jnp.maximum(pos - w // 2, 0)
        p = tot / cnt.astype(F32) - u
        y = jnp.dot(p.astype(BF16), pw_ref[k], preferred_element_type=F32) * ps_ref[:, c0:c0 + POOL_GROUP]
        gb_ref[:, c0:c0 + POOL_GROUP] = (y * szb_ref[:, c0:c0 + POOL_GROUP].astype(F32)).astype(BF16)

    yc = yc_ref[...]
    mu = jnp.mean(yc, axis=-1, keepdims=True)
    dev = yc - mu
    var = jnp.mean(dev * dev, axis=-1, keepdims=True)
    t = _silu(dev * lax.rsqrt(var + EPS) * lng_ref[...] + lnb_ref[...])
    y = jnp.dot(t.astype(BF16), pwc_ref[...], preferred_element_type=F32) + pwcb_ref[...]
    gc_ref[...] = (y * szc_ref[...].astype(F32)).astype(BF16)


def _pool_tail(lin, act, yconv, layer, pool_w, pool_scale, ln_g, ln_b, pwc_w, pwc_b, tm, n_row):
    T = lin.shape[0]
    col = lambda blk: pl.BlockSpec((tm, BRANCH_W), lambda i: (i, blk))
    vec = pl.BlockSpec((1, BRANCH_W), lambda i: (0, 0))
    return pl.pallas_call(
        functools.partial(_pool_tail_kernel, tm=tm, n_row=n_row),
        out_shape=(jax.ShapeDtypeStruct((T, BRANCH_W), BF16), jax.ShapeDtypeStruct((T, BRANCH_W), BF16)),
        grid=(T // tm,),
        in_specs=[
            col(1), col(1), col(0), col(2),
            pl.BlockSpec((None,) + pool_w.shape[1:], lambda i: (layer, 0, 0, 0)),
            vec, vec, vec,
            pl.BlockSpec((None, BRANCH_W, BRANCH_W), lambda i: (layer, 0, 0)),
            vec,
        ],
        out_specs=(col(0), col(0)),
        compiler_params=_params("parallel"),
        name="pool_tail",
    )(lin, act, yconv, act, pool_w, pool_scale, ln_g, ln_b, pwc_w, pwc_b)


def _merge_kernel(ya_ref, sza_ref, gb_ref, gc_ref, w_ref, gt0_ref, gt1_ref, gt2_ref, o_ref):
    ga = (ya_ref[...] * sza_ref[...].astype(F32)).astype(BF16)
    acc = None
    for k, (g, gt_ref) in enumerate(((ga, gt0_ref), (gb_ref[...], gt1_ref), (gc_ref[...], gt2_ref))):
        t = gt_ref[...].astype(F32) * jnp.dot(g, w_ref[k], preferred_element_type=F32)
        acc = t if acc is None else acc + t
    o_ref[...] = acc.astype(BF16)


def _merge(ya, gb, gc, w_bout, layer, act, tm, tn):
    T = gb.shape[0]
    g_spec = pl.BlockSpec((tm, BRANCH_W), lambda i, j: (i, 0))
    gate = lambda k: pl.BlockSpec((tm, tn), lambda i, j, o=(3 * BRANCH_W + k * D_MODEL) // tn: (i, o + j))
    return pl.pallas_call(
        _merge_kernel,
        out_shape=jax.ShapeDtypeStruct((T, D_MODEL), BF16),
        grid=(T // tm, D_MODEL // tn),
        in_specs=[
            g_spec, g_spec, g_spec, g_spec,
            pl.BlockSpec((None, N_BRANCH, BRANCH_W, tn), lambda i, j: (layer, 0, 0, j)),
            gate(0), gate(1), gate(2),
        ],
        out_specs=pl.BlockSpec((tm, tn), lambda i, j: (i, j)),
        compiler_params=_params("parallel", "parallel"),
        name="merge",
    )(ya, act, gb, gc, w_bout, act, act, act)


def _out_mid_kernel(acc_ref, x_ref, w_ref, b_ref, gate_ref, g_ref, sc_ref, sh_ref, xo_ref, h_ref):
    out = jnp.dot(acc_ref[...], w_ref[...], preferred_element_type=F32) + b_ref[...]
    xn = x_ref[...] + gate_ref[...] * out
    xo_ref[...] = xn
    h_ref[...] = _norm_mod(xn, g_ref[...], sc_ref[...], sh_ref[...]).astype(BF16)


def _out_last_kernel(acc_ref, x_ref, w_ref, b_ref, gate_ref, g_ref, o_ref):
    out = jnp.dot(acc_ref[...], w_ref[...], preferred_element_type=F32) + b_ref[...]
    xn = x_ref[...] + gate_ref[...] * out
    o_ref[...] = xn * lax.rsqrt(jnp.mean(xn * xn, axis=-1, keepdims=True) + EPS) * g_ref[...]


def _out_proj(acc3, x3, w_out, layer, b_out, gate, g_next, scale_next, shift_next, tm, last):
    G, T, d = x3.shape
    tile = pl.BlockSpec((None, tm, d), lambda gi, i: (gi, i, 0))
    mod = pl.BlockSpec((None, 1, d), lambda gi, i: (gi, 0, 0))
    vec = pl.BlockSpec((1, d), lambda gi, i: (0, 0))
    wspec = pl.BlockSpec((None, d, d), lambda gi, i: (layer, 0, 0))
    if last:
        return pl.pallas_call(
            _out_last_kernel,
            out_shape=jax.ShapeDtypeStruct((G, T, d), F32),
            grid=(G, T // tm),
            in_specs=[tile, tile, wspec, vec, mod, vec],
            out_specs=tile,
            compiler_params=_params("parallel", "parallel"),
            name="out_last",
        )(acc3, x3, w_out, b_out, gate, g_next)
    return pl.pallas_call(
        _out_mid_kernel,
        out_shape=(jax.ShapeDtypeStruct((G, T, d), F32), jax.ShapeDtypeStruct((G, T, d), BF16)),
        grid=(G, T // tm),
        in_specs=[tile, tile, wspec, vec, mod, vec, mod, mod],
        out_specs=(tile, tile),
        compiler_params=_params("parallel", "parallel"),
        name="out_mid",
    )(acc3, x3, w_out, b_out, gate, g_next, scale_next, shift_next)


def _mix(h, wts, lw, layer, B, L, n_row, h0, state_only, tm_in, tm_tail, tm_merge):
    T = B * L
    lin = _in_proj("lin", h, wts["w_in"], layer, 1 if state_only else 2, tm_in)
    ya, hfin = _rglru(lin.reshape(B, L, -1), lw["conv_a_w"], lw["conv_a_b"], wts["wri"][layer],
                      wts["bri"][layer], lw["lam"], h0)
    if state_only:
        return None, hfin
    act = _in_proj("act", h, wts["w_in"], layer, N_ACT_BLOCKS, tm_in)
    y = _in_proj("glu", h, wts["w_in"], layer, 1, tm_in)
    yconv = _dwconv(y.reshape(B, L, BRANCH_W), lw["convc_w"], lw["convc_b"])
    gb, gc = _pool_tail(lin, act, yconv.reshape(T, BRANCH_W), layer, wts["pool_w"], lw["pool_scale"],
                        lw["lnc_g"], lw["lnc_b"], wts["pwc_w"], lw["pwc_b"], tm_tail, n_row)
    acc = _merge(ya.reshape(T, BRANCH_W), gb, gc, wts["w_bout"], layer, act, tm_merge, 512)
    return acc, hfin


def kernel(x, c, ctx, c_ctx, norm_g, w_ada, b_ada, w_in, conv_a_w, conv_a_b, lru_wr, lru_br, lru_wi, lru_bi,
           lru_lambda, pool_w, pool_scale, convc_w, convc_b, lnc_g, lnc_b, pwc_w, pwc_b, w_bout, w_out, b_out,
           final_g):
    B, L, d = x.shape
    CL = ctx.shape[1]
    depth = w_in.shape[0]
    TC = B * CL

    rows = -(-(B + 1) // SUBLANES) * SUBLANES
    cc = jnp.zeros((rows, d), F32).at[:B].set(c).at[B].set(c_ctx)
    mod = _ada(cc, w_ada, b_ada)

    def mods(l):
        sh, sc, gt = (mod[l, :, k * d:(k + 1) * d] for k in range(3))
        lat = tuple(v[:B].reshape(B, 1, d) for v in (sh, sc, gt))
        cx = tuple(v[B:B + 1].reshape(1, 1, d) for v in (sh, sc, gt))
        return lat, cx

    wts = {
        "w_in": w_in.astype(BF16),
        "wri": jnp.concatenate([lru_wr[:, 0], lru_wi[:, 0], lru_wr[:, 1], lru_wi[:, 1]], axis=-1).astype(BF16),
        "bri": jnp.concatenate([lru_br[:, 0], lru_bi[:, 0], lru_br[:, 1], lru_bi[:, 1]],
                               axis=-1).reshape(depth, LRU_HEADS, 1, 4 * HEAD_DIM),
        "pool_w": pool_w.astype(BF16),
        "pwc_w": pwc_w.astype(BF16),
        "w_bout": w_bout.astype(BF16),
        "w_out": w_out.astype(BF16),
    }

    def layer_params(l):
        return {
            "conv_a_w": conv_a_w[l], "conv_a_b": conv_a_b[l].reshape(1, -1),
            "lam": lru_lambda[l],
            "pool_scale": pool_scale[l].reshape(1, -1),
            "convc_w": convc_w[l], "convc_b": convc_b[l].reshape(1, -1),
            "lnc_g": lnc_g[l].reshape(1, -1), "lnc_b": lnc_b[l].reshape(1, -1),
            "pwc_b": pwc_b[l].reshape(1, -1),
            "b_out": b_out[l].reshape(1, -1),
        }

    tm_lat = min(L, 512)
    tm_ctx = min(TC, 512)
    tm_in = min(B * L, 1024)
    x3 = x
    xc3 = ctx.reshape(1, TC, d)
    (sh, sc, _), (shc, scc, _) = mods(0)
    h = _norm(x3, norm_g[0].reshape(1, d), sc, sh, tm_lat)
    hc = _norm(xc3, norm_g[0].reshape(1, d), scc, shc, tm_ctx)
    zeros_h0 = jnp.zeros((B, 2, BRANCH_W), F32)

    for l in range(depth):
        last = l == depth - 1
        lw = layer_params(l)
        (_, _, gate), (_, _, gate_c) = mods(l)
        acc_c, hfin = _mix(hc.reshape(TC, d), wts, lw, l, B, CL, CL, zeros_h0, last, tm_ctx, CL, tm_ctx)
        acc, _ = _mix(h.reshape(B * L, d), wts, lw, l, B, L, GRID_W, hfin, False, tm_in, tm_lat,
                      min(B * L, 1024))
        if last:
            return _out_proj(acc.reshape(B, L, d), x3, wts["w_out"], l, lw["b_out"], gate,
                             final_g.reshape(1, d), None, None, tm_lat, True)
        (sh, sc, _), (shc, scc, _) = mods(l + 1)
        g_next = norm_g[l + 1].reshape(1, d)
        x3, h = _out_proj(acc.reshape(B, L, d), x3, wts["w_out"], l, lw["b_out"], gate, g_next, sc, sh,
                          tm_lat, False)
        xc3, hc = _out_proj(acc_c.reshape(1, TC, d), xc3, wts["w_out"], l, lw["b_out"], gate_c, g_next, scc,
                            shc, tm_ctx, False)
```

```python
import functools

import jax
import jax.numpy as jnp
from jax import lax
from jax.experimental import pallas as pl
from jax.experimental.pallas import tpu as pltpu

F32 = jnp.float32
BF16 = jnp.bfloat16

D_MODEL = 2048
BRANCH_W = 1024
LRU_HEADS = 8
HEAD_DIM = BRANCH_W // LRU_HEADS
LRU_C = 8.0
CONV_A_WIDTH = 4
CONV_A_PAD_LEFT = 2
GRID_W = 64
POOL_WINDOWS = (2, 4, 8, 16)
POOL_GROUP = BRANCH_W // len(POOL_WINDOWS)
CONV_C_WIDTH = 31
CONV_C_PAD_LEFT = (CONV_C_WIDTH - 1) // 2
N_BRANCH = 3
EPS = 1e-6
BLK_XA, BLK_ZA, BLK_XB, BLK_ZB, BLK_CV, BLK_CG, BLK_ZC, BLK_GL = range(8)
N_ACT_BLOCKS = 3 + N_BRANCH * D_MODEL // BRANCH_W

SUBLANES = 8
V7X_VMEM_LIMIT_BYTES = 56 * 1024 * 1024


def _params(*sem):
    return pltpu.CompilerParams(dimension_semantics=sem, vmem_limit_bytes=V7X_VMEM_LIMIT_BYTES)


def _silu(v):
    return v * jax.nn.sigmoid(v)


def _norm_mod(x, g, scale, shift):
    y = x * lax.rsqrt(jnp.mean(x * x, axis=-1, keepdims=True) + EPS) * g
    return y * (1.0 + scale) + shift


def _ada_kernel(c_ref, w_ref, b_ref, o_ref):
    s = _silu(c_ref[...])
    o_ref[...] = jnp.dot(s.astype(BF16), w_ref[...].astype(BF16), preferred_element_type=F32) + b_ref[...]


def _ada(cc, w_ada, b_ada, tn=512):
    depth, d, n = w_ada.shape
    rows = cc.shape[0]
    return pl.pallas_call(
        _ada_kernel,
        out_shape=jax.ShapeDtypeStruct((depth, rows, n), F32),
        grid=(depth, n // tn),
        in_specs=[
            pl.BlockSpec((rows, d), lambda l, j: (0, 0)),
            pl.BlockSpec((None, d, tn), lambda l, j: (l, 0, j)),
            pl.BlockSpec((None, 1, tn), lambda l, j: (l, 0, j)),
        ],
        out_specs=pl.BlockSpec((None, rows, tn), lambda l, j: (l, 0, j)),
        compiler_params=_params("parallel", "parallel"),
        name="ada_mod",
    )(cc, w_ada, b_ada.reshape(depth, 1, n))


def _norm_kernel(x_ref, g_ref, sc_ref, sh_ref, h_ref):
    h_ref[...] = _norm_mod(x_ref[...], g_ref[...], sc_ref[...], sh_ref[...]).astype(BF16)


def _norm(x3, g, scale, shift, tm):
    G, T, d = x3.shape
    mod_spec = pl.BlockSpec((None, 1, d), lambda gi, i: (gi, 0, 0))
    return pl.pallas_call(
        _norm_kernel,
        out_shape=jax.ShapeDtypeStruct((G, T, d), BF16),
        grid=(G, T // tm),
        in_specs=[
            pl.BlockSpec((None, tm, d), lambda gi, i: (gi, i, 0)),
            pl.BlockSpec((1, d), lambda gi, i: (0, 0)),
            mod_spec, mod_spec,
        ],
        out_specs=pl.BlockSpec((None, tm, d), lambda gi, i: (gi, i, 0)),
        compiler_params=_params("parallel", "parallel"),
        name="norm_mod",
    )(x3, g, scale, shift)


IN_PROJ_TN = 512
_SUB = BRANCH_W // IN_PROJ_TN


def _lin_kernel(h_ref, w_ref, o_ref):
    o_ref[...] = jnp.dot(h_ref[...], w_ref[...].astype(BF16), preferred_element_type=F32)


def _glu_kernel(h_ref, wv_ref, wg_ref, o_ref):
    h = h_ref[...]
    v = jnp.dot(h, wv_ref[...].astype(BF16), preferred_element_type=F32)
    g = jnp.dot(h, wg_ref[...].astype(BF16), preferred_element_type=F32)
    o_ref[...] = v * jax.nn.sigmoid(g)


def _act_kernel(h_ref, w_ref, o_ref):
    acc = jnp.dot(h_ref[...], w_ref[...].astype(BF16), preferred_element_type=F32)
    s = jax.nn.sigmoid(acc)
    o_ref[...] = jnp.where(pl.program_id(1) < 3 * _SUB, acc * s, s).astype(BF16)


def _w_spec(layer, d, blk_of_q):
    return pl.BlockSpec((None, d, IN_PROJ_TN), lambda i, j: (layer, 0, blk_of_q(j // _SUB) * _SUB + j % _SUB))


def _in_proj(kind, h, w_in, layer, n_blocks, tm):
    T, d = h.shape
    h_spec = pl.BlockSpec((tm, d), lambda i, j: (i, 0))
    if kind == "lin":
        body, dtype = _lin_kernel, F32
        w_specs = [_w_spec(layer, d, lambda q: (BLK_XB - BLK_XA) * q)]
    elif kind == "glu":
        body, dtype = _glu_kernel, F32
        w_specs = [_w_spec(layer, d, lambda q: BLK_CV + q), _w_spec(layer, d, lambda q: BLK_CG + q)]
    else:
        body, dtype = _act_kernel, BF16
        w_specs = [_w_spec(layer, d, lambda q: jnp.where(q < 3, 1 + 2 * q + q // 2, q + BLK_GL - 3))]
    return pl.pallas_call(
        body,
        out_shape=jax.ShapeDtypeStruct((T, n_blocks * BRANCH_W), dtype),
        grid=(T // tm, n_blocks * _SUB),
        in_specs=[h_spec] + w_specs,
        out_specs=pl.BlockSpec((tm, IN_PROJ_TN), lambda i, j: (i, j)),
        compiler_params=_params("parallel", "parallel"),
        name="in_proj_" + kind,
    )(h, *([w_in] * len(w_specs)))


def _rglru_kernel(xa_ref, cw_ref, cb_ref, wri_ref, bri_ref, lam_ref, h0_ref,
                  ya_ref, hfin_ref, ua_s, hf_s, af_s, hb_s, ab_s, *, L, JB):
    S = L // SUBLANES
    xa = xa_ref[...]
    row = lax.broadcasted_iota(jnp.int32, (L, HEAD_DIM), 0)
    ua = jnp.zeros((L, HEAD_DIM), F32) + cb_ref[...]
    for k in range(CONV_A_WIDTH):
        o = k - CONV_A_PAD_LEFT
        sh = xa if o == 0 else pltpu.roll(xa, (-o) % L, axis=0)
        valid = (row + o >= 0) & (row + o < L)
        ua = ua + cw_ref[k:k + 1, :] * jnp.where(valid, sh, 0.0)
    ua_s[...] = ua

    nlam = -lam_ref[...]
    sp = jnp.maximum(nlam, 0.0) + jnp.log1p(jnp.exp(-jnp.abs(nlam)))
    rate = -LRU_C * sp

    def seg_rows(j):
        return (pl.ds(j, SUBLANES, stride=S), slice(None))

    def coeffs(j0, d):
        u = jnp.concatenate([ua_s[seg_rows(j0 + v)] for v in range(JB)], axis=0)
        c0 = 2 * d * HEAD_DIM
        g = (jnp.dot(u.astype(BF16), wri_ref[:, c0:c0 + 2 * HEAD_DIM], preferred_element_type=F32)
             + bri_ref[:, c0:c0 + 2 * HEAD_DIM])
        r = jax.nn.sigmoid(g[:, :HEAD_DIM])
        ig = jax.nn.sigmoid(g[:, HEAD_DIM:])
        log_a = r * rate[d:d + 1, :]
        a = jnp.exp(log_a)
        mult = jnp.sqrt(-jnp.tanh(log_a) * (a * a + 1.0))
        return a, mult * ig * u

    def block(i, carry):
        hf, af, hb, ab = carry
        j0 = i * JB
        a, b = coeffs(j0, 0)
        for v in range(JB):
            av = a[v * SUBLANES:(v + 1) * SUBLANES]
            hf = av * hf + b[v * SUBLANES:(v + 1) * SUBLANES]
            af = av * af
            hf_s[j0 + v] = hf
            af_s[j0 + v] = af
        j0 = S - (i + 1) * JB
        a, b = coeffs(j0, 1)
        for v in reversed(range(JB)):
            av = a[v * SUBLANES:(v + 1) * SUBLANES]
            hb = av * hb + b[v * SUBLANES:(v + 1) * SUBLANES]
            ab = av * ab
            hb_s[j0 + v] = hb
            ab_s[j0 + v] = ab
        return hf, af, hb, ab

    zero = jnp.zeros((SUBLANES, HEAD_DIM), F32)
    one = jnp.ones((SUBLANES, HEAD_DIM), F32)
    hf, af, hb, ab = lax.fori_loop(0, S // JB, block, (zero, one, zero, one))

    cf = [h0_ref[0:1, :]]
    for s in range(SUBLANES):
        cf.append(hf[s:s + 1] + af[s:s + 1] * cf[-1])
    cb = [h0_ref[1:2, :]]
    for s in reversed(range(SUBLANES)):
        cb.append(hb[s:s + 1] + ab[s:s + 1] * cb[-1])
    cb = cb[::-1]
    hfin_ref[0:1, :] = cf[SUBLANES]
    hfin_ref[1:2, :] = cb[0]
    enter_f = jnp.concatenate(cf[:SUBLANES], axis=0)
    enter_b = jnp.concatenate(cb[1:], axis=0)

    def finish(i, _):
        for v in range(JB):
            j = i * JB + v
            ya_ref[seg_rows(j)] = (hf_s[j] + af_s[j] * enter_f) + (hb_s[j] + ab_s[j] * enter_b)
        return 0

    lax.fori_loop(0, S // JB, finish, 0)


def _rglru(lin3, conv_w, conv_b, wri, bri, lam, h0):
    B, L, _ = lin3.shape
    S = L // SUBLANES
    JB = min(S, 32)
    col = pl.BlockSpec((None, L, HEAD_DIM), lambda b, h: (b, 0, h))
    scr = pltpu.VMEM((S, SUBLANES, HEAD_DIM), F32)
    return pl.pallas_call(
        functools.partial(_rglru_kernel, L=L, JB=JB),
        out_shape=(jax.ShapeDtypeStruct((B, L, BRANCH_W), F32),
                   jax.ShapeDtypeStruct((B, 2, BRANCH_W), F32)),
        grid=(B, LRU_HEADS),
        in_specs=[
            col,
            pl.BlockSpec((CONV_A_WIDTH, HEAD_DIM), lambda b, h: (0, h)),
            pl.BlockSpec((1, HEAD_DIM), lambda b, h: (0, h)),
            pl.BlockSpec((None, HEAD_DIM, 4 * HEAD_DIM), lambda b, h: (h, 0, 0)),
            pl.BlockSpec((None, 1, 4 * HEAD_DIM), lambda b, h: (h, 0, 0)),
            pl.BlockSpec((2, HEAD_DIM), lambda b, h: (0, h)),
            pl.BlockSpec((None, 2, HEAD_DIM), lambda b, h: (b, 0, h)),
        ],
        out_specs=(col, pl.BlockSpec((None, 2, HEAD_DIM), lambda b, h: (b, 0, h))),
        scratch_shapes=[pltpu.VMEM((L, HEAD_DIM), F32)] + [scr] * 4,
        compiler_params=_params("parallel", "parallel"),
        name="rglru",
    )(lin3, conv_w, conv_b, wri, bri, lam, h0)


_CONVC_HALO = 16


def _dwconv_kernel(y_ref, w_ref, b_ref, o_ref, s_ref, *, L, RB):
    LP = L + 2 * _CONVC_HALO
    s_ref[0:_CONVC_HALO, :] = jnp.zeros((_CONVC_HALO, HEAD_DIM), F32)
    s_ref[L + _CONVC_HALO:LP, :] = jnp.zeros((_CONVC_HALO, HEAD_DIM), F32)
    s_ref[_CONVC_HALO:L + _CONVC_HALO, :] = y_ref[...]

    first = _CONVC_HALO - CONV_C_PAD_LEFT

    def blk(i, _):
        r0 = pl.multiple_of(i * RB, RB)
        acc = jnp.zeros((RB, HEAD_DIM), F32) + b_ref[...]
        for k in range(CONV_C_WIDTH):
            acc = acc + w_ref[k:k + 1, :] * s_ref[pl.ds(r0 + (first + k), RB), :]
        o_ref[pl.ds(r0, RB), :] = acc
        return 0

    lax.fori_loop(0, L // RB, blk, 0)


def _dwconv(y3, w, b):
    B, L, _ = y3.shape
    RB = min(L, 256)
    col = pl.BlockSpec((None, L, HEAD_DIM), lambda bi, h: (bi, 0, h))
    return pl.pallas_call(
        functools.partial(_dwconv_kernel, L=L, RB=RB),
        out_shape=jax.ShapeDtypeStruct((B, L, BRANCH_W), F32),
        grid=(B, BRANCH_W // HEAD_DIM),
        in_specs=[
            col,
            pl.BlockSpec((CONV_C_WIDTH, HEAD_DIM), lambda bi, h: (0, h)),
            pl.BlockSpec((1, HEAD_DIM), lambda bi, h: (0, h)),
        ],
        out_specs=col,
        scratch_shapes=[pltpu.VMEM((L + 2 * _CONVC_HALO, HEAD_DIM), F32)],
        compiler_params=_params("parallel", "parallel"),
        name="dwconv",
    )(y3, w, b)


def _pool_tail_kernel(xb_ref, szb_ref, yc_ref, szc_ref, pw_ref, ps_ref, lng_ref, lnb_ref, pwc_ref, pwcb_ref,
                      gb_ref, gc_ref, *, tm, n_row):
    pos = lax.broadcasted_iota(jnp.int32, (tm, POOL_GROUP), 0) % n_row
    for k, w in enumerate(POOL_WINDOWS):
        c0 = k * POOL_GROUP
        u = xb_ref[:, c0:c0 + POOL_GROUP]
        tot = u
        for o in range(-(w // 2), w // 2):
            if o == 0:
                continue
            sh = pltpu.roll(u, (-o) % tm, axis=0)
            tot = tot + jnp.where((pos + o >= 0) & (pos + o < n_row), sh, 0.0)
        cnt = jnp.minimum(pos + w // 2, n_row) - jnp.maximum(pos - w // 2, 0)
        p = tot / cnt.astype(F32) - u
        y = jnp.dot(p.astype(BF16), pw_ref[k], preferred_element_type=F32) * ps_ref[:, c0:c0 + POOL_GROUP]
        gb_ref[:, c0:c0 + POOL_GROUP] = (y * szb_ref[:, c0:c0 + POOL_GROUP].astype(F32)).astype(BF16)

    yc = yc_ref[...]
    mu = jnp.mean(yc, axis=-1, keepdims=True)
    dev = yc - mu
    var = jnp.mean(dev * dev, axis=-1, keepdims=True)
    t = _silu(dev * lax.rsqrt(var + EPS) * lng_ref[...] + lnb_ref[...])
    y = jnp.dot(t.astype(BF16), pwc_ref[...], preferred_element_type=F32) + pwcb_ref[...]
    gc_ref[...] = (y * szc_ref[...].astype(F32)).astype(BF16)


def _pool_tail(lin, act, yconv, layer, pool_w, pool_scale, ln_g, ln_b, pwc_w, pwc_b, tm, n_row):
    T = lin.shape[0]
    col = lambda blk: pl.BlockSpec((tm, BRANCH_W), lambda i: (i, blk))
    vec = pl.BlockSpec((1, BRANCH_W), lambda i: (0, 0))
    return pl.pallas_call(
        functools.partial(_pool_tail_kernel, tm=tm, n_row=n_row),
        out_shape=(jax.ShapeDtypeStruct((T, BRANCH_W), BF16), jax.ShapeDtypeStruct((T, BRANCH_W), BF16)),
        grid=(T // tm,),
        in_specs=[
            col(1), col(1), col(0), col(2),
            pl.BlockSpec((None,) + pool_w.shape[1:], lambda i: (layer, 0, 0, 0)),
            vec, vec, vec,
            pl.BlockSpec((None, BRANCH_W, BRANCH_W), lambda i: (layer, 0, 0)),
            vec,
        ],
        out_specs=(col(0), col(0)),
        compiler_params=_params("parallel"),
        name="pool_tail",
    )(lin, act, yconv, act, pool_w, pool_scale, ln_g, ln_b, pwc_w, pwc_b)


def _merge_kernel(ya_ref, sza_ref, gb_ref, gc_ref, w_ref, gt0_ref, gt1_ref, gt2_ref, o_ref):
    ga = (ya_ref[...] * sza_ref[...].astype(F32)).astype(BF16)
    acc = None
    for k, (g, gt_ref) in enumerate(((ga, gt0_ref), (gb_ref[...], gt1_ref), (gc_ref[...], gt2_ref))):
        t = gt_ref[...].astype(F32) * jnp.dot(g, w_ref[k], preferred_element_type=F32)
        acc = t if acc is None else acc + t
    o_ref[...] = acc.astype(BF16)


def _merge(ya, gb, gc, w_bout, layer, act, tm, tn):
    T = gb.shape[0]
    g_spec = pl.BlockSpec((tm, BRANCH_W), lambda i, j: (i, 0))
    gate = lambda k: pl.BlockSpec((tm, tn), lambda i, j, o=(3 * BRANCH_W + k * D_MODEL) // tn: (i, o + j))
    return pl.pallas_call(
        _merge_kernel,
        out_shape=jax.ShapeDtypeStruct((T, D_MODEL), BF16),
        grid=(T // tm, D_MODEL // tn),
        in_specs=[
            g_spec, g_spec, g_spec, g_spec,
            pl.BlockSpec((None, N_BRANCH, BRANCH_W, tn), lambda i, j: (layer, 0, 0, j)),
            gate(0), gate(1), gate(2),
        ],
        out_specs=pl.BlockSpec((tm, tn), lambda i, j: (i, j)),
        compiler_params=_params("parallel", "parallel"),
        name="merge",
    )(ya, act, gb, gc, w_bout, act, act, act)


def _out_mid_kernel(acc_ref, x_ref, w_ref, b_ref, gate_ref, g_ref, sc_ref, sh_ref, xo_ref, h_ref):
    out = jnp.dot(acc_ref[...], w_ref[...], preferred_element_type=F32) + b_ref[...]
    xn = x_ref[...] + gate_ref[...] * out
    xo_ref[...] = xn
    h_ref[...] = _norm_mod(xn, g_ref[...], sc_ref[...], sh_ref[...]).astype(BF16)


def _out_last_kernel(acc_ref, x_ref, w_ref, b_ref, gate_ref, g_ref, o_ref):
    out = jnp.dot(acc_ref[...], w_ref[...], preferred_element_type=F32) + b_ref[...]
    xn = x_ref[...] + gate_ref[...] * out
    o_ref[...] = xn * lax.rsqrt(jnp.mean(xn * xn, axis=-1, keepdims=True) + EPS) * g_ref[...]


def _out_proj(acc3, x3, w_out, layer, b_out, gate, g_next, scale_next, shift_next, tm, last):
    G, T, d = x3.shape
    tile = pl.BlockSpec((None, tm, d), lambda gi, i: (gi, i, 0))
    mod = pl.BlockSpec((None, 1, d), lambda gi, i: (gi, 0, 0))
    vec = pl.BlockSpec((1, d), lambda gi, i: (0, 0))
    wspec = pl.BlockSpec((None, d, d), lambda gi, i: (layer, 0, 0))
    if last:
        return pl.pallas_call(
            _out_last_kernel,
            out_shape=jax.ShapeDtypeStruct((G, T, d), F32),
            grid=(G, T // tm),
            in_specs=[tile, tile, wspec, vec, mod, vec],
            out_specs=tile,
            compiler_params=_params("parallel", "parallel"),
            name="out_last",
        )(acc3, x3, w_out, b_out, gate, g_next)
    return pl.pallas_call(
        _out_mid_kernel,
        out_shape=(jax.ShapeDtypeStruct((G, T, d), F32), jax.ShapeDtypeStruct((G, T, d), BF16)),
        grid=(G, T // tm),
        in_specs=[tile, tile, wspec, vec, mod, vec, mod, mod],
        out_specs=(tile, tile),
        compiler_params=_params("parallel", "parallel"),
        name="out_mid",
    )(acc3, x3, w_out, b_out, gate, g_next, scale_next, shift_next)


def _mix(h, wts, lw, layer, B, L, n_row, h0, state_only, tm_in, tm_tail, tm_merge):
    T = B * L
    lin = _in_proj("lin", h, wts["w_in"], layer, 1 if state_only else 2, tm_in)
    ya, hfin = _rglru(lin.reshape(B, L, -1), lw["conv_a_w"], lw["conv_a_b"], wts["wri"][layer],
                      wts["bri"][layer], lw["lam"], h0)
    if state_only:
        return None, hfin
    act = _in_proj("act", h, wts["w_in"], layer, N_ACT_BLOCKS, tm_in)
    y = _in_proj("glu", h, wts["w_in"], layer, 1, tm_in)
    yconv = _dwconv(y.reshape(B, L, BRANCH_W), lw["convc_w"], lw["convc_b"])
    gb, gc = _pool_tail(lin, act, yconv.reshape(T, BRANCH_W), layer, wts["pool_w"], lw["pool_scale"],
                        lw["lnc_g"], lw["lnc_b"], wts["pwc_w"], lw["pwc_b"], tm_tail, n_row)
    acc = _merge(ya.reshape(T, BRANCH_W), gb, gc, wts["w_bout"], layer, act, tm_merge, 512)
    return acc, hfin


def kernel(x, c, ctx, c_ctx, norm_g, w_ada, b_ada, w_in, conv_a_w, conv_a_b, lru_wr, lru_br, lru_wi, lru_bi,
           lru_lambda, pool_w, pool_scale, convc_w, convc_b, lnc_g, lnc_b, pwc_w, pwc_b, w_bout, w_out, b_out,
           final_g):
    B, L, d = x.shape
    CL = ctx.shape[1]
    depth = w_in.shape[0]
    TC = B * CL

    rows = -(-(B + 1) // SUBLANES) * SUBLANES
    cc = jnp.zeros((rows, d), F32).at[:B].set(c).at[B].set(c_ctx)
    mod = _ada(cc, w_ada, b_ada)

    def mods(l):
        sh, sc, gt = (mod[l, :, k * d:(k + 1) * d] for k in range(3))
        lat = tuple(v[:B].reshape(B, 1, d) for v in (sh, sc, gt))
        cx = tuple(v[B:B + 1].reshape(1, 1, d) for v in (sh, sc, gt))
        return lat, cx

    wts = {
        "w_in": w_in,
        "wri": jnp.concatenate([lru_wr[:, 0], lru_wi[:, 0], lru_wr[:, 1], lru_wi[:, 1]], axis=-1).astype(BF16),
        "bri": jnp.concatenate([lru_br[:, 0], lru_bi[:, 0], lru_br[:, 1], lru_bi[:, 1]],
                               axis=-1).reshape(depth, LRU_HEADS, 1, 4 * HEAD_DIM),
        "pool_w": pool_w.astype(BF16),
        "pwc_w": pwc_w.astype(BF16),
        "w_bout": w_bout.astype(BF16),
        "w_out": w_out.astype(BF16),
    }

    def layer_params(l):
        return {
            "conv_a_w": conv_a_w[l], "conv_a_b": conv_a_b[l].reshape(1, -1),
            "lam": lru_lambda[l],
            "pool_scale": pool_scale[l].reshape(1, -1),
            "convc_w": convc_w[l], "convc_b": convc_b[l].reshape(1, -1),
            "lnc_g": lnc_g[l].reshape(1, -1), "lnc_b": lnc_b[l].reshape(1, -1),
            "pwc_b": pwc_b[l].reshape(1, -1),
            "b_out": b_out[l].reshape(1, -1),
        }

    tm_lat = min(L, 512)
    tm_ctx = min(TC, 512)
    tm_in = min(B * L, 2048)
    x3 = x
    xc3 = ctx.reshape(1, TC, d)
    (sh, sc, _), (shc, scc, _) = mods(0)
    h = _norm(x3, norm_g[0].reshape(1, d), sc, sh, tm_lat)
    hc = _norm(xc3, norm_g[0].reshape(1, d), scc, shc, tm_ctx)
    zeros_h0 = jnp.zeros((B, 2, BRANCH_W), F32)

    for l in range(depth):
        last = l == depth - 1
        lw = layer_params(l)
        (_, _, gate), (_, _, gate_c) = mods(l)
        acc_c, hfin = _mix(hc.reshape(TC, d), wts, lw, l, B, CL, CL, zeros_h0, last, tm_ctx, CL, tm_ctx)
        acc, _ = _mix(h.reshape(B * L, d), wts, lw, l, B, L, GRID_W, hfin, False, tm_in, tm_lat,
                      min(B * L, 1024))
        if last:
            return _out_proj(acc.reshape(B, L, d), x3, wts["w_out"], l, lw["b_out"], gate,
                             final_g.reshape(1, d), None, None, tm_lat, True)
        (sh, sc, _), (shc, scc, _) = mods(l + 1)
        g_next = norm_g[l + 1].reshape(1, d)
        x3, h = _out_proj(acc.reshape(B, L, d), x3, wts["w_out"], l, lw["b_out"], gate, g_next, sc, sh,
                          tm_lat, False)
        xc3, hc = _out_proj(acc_c.reshape(1, TC, d), xc3, wts["w_out"], l, lw["b_out"], gate_c, g_next, scc,
                            shc, tm_ctx, False)
```

```python
import functools

import jax
import jax.numpy as jnp
from jax import lax
from jax.experimental import pallas as pl
from jax.experimental.pallas import tpu as pltpu

F32 = jnp.float32
BF16 = jnp.bfloat16

D_MODEL = 2048
BRANCH_W = 1024
LRU_HEADS = 8
HEAD_DIM = BRANCH_W // LRU_HEADS
LRU_C = 8.0
CONV_A_WIDTH = 4
CONV_A_PAD_LEFT = 2
GRID_W = 64
POOL_WINDOWS = (2, 4, 8, 16)
POOL_GROUP = BRANCH_W // len(POOL_WINDOWS)
CONV_C_WIDTH = 31
CONV_C_PAD_LEFT = (CONV_C_WIDTH - 1) // 2
N_BRANCH = 3
EPS = 1e-6
BLK_XA, BLK_ZA, BLK_XB, BLK_ZB, BLK_CV, BLK_CG, BLK_ZC, BLK_GL = range(8)
N_ACT_BLOCKS = 3 + N_BRANCH * D_MODEL // BRANCH_W

SUBLANES = 8
V7X_VMEM_LIMIT_BYTES = 56 * 1024 * 1024


def _params(*sem):
    return pltpu.CompilerParams(dimension_semantics=sem, vmem_limit_bytes=V7X_VMEM_LIMIT_BYTES)


def _silu(v):
    return v * jax.nn.sigmoid(v)


def _norm_mod(x, g, scale, shift):
    y = x * lax.rsqrt(jnp.mean(x * x, axis=-1, keepdims=True) + EPS) * g
    return y * (1.0 + scale) + shift


def _ada_kernel(c_ref, w_ref, b_ref, o_ref):
    s = _silu(c_ref[...])
    o_ref[...] = jnp.dot(s.astype(BF16), w_ref[...].astype(BF16), preferred_element_type=F32) + b_ref[...]


def _ada(cc, w_ada, b_ada, tn=512):
    depth, d, n = w_ada.shape
    rows = cc.shape[0]
    return pl.pallas_call(
        _ada_kernel,
        out_shape=jax.ShapeDtypeStruct((depth, rows, n), F32),
        grid=(depth, n // tn),
        in_specs=[
            pl.BlockSpec((rows, d), lambda l, j: (0, 0)),
            pl.BlockSpec((None, d, tn), lambda l, j: (l, 0, j)),
            pl.BlockSpec((None, 1, tn), lambda l, j: (l, 0, j)),
        ],
        out_specs=pl.BlockSpec((None, rows, tn), lambda l, j: (l, 0, j)),
        compiler_params=_params("parallel", "parallel"),
        name="ada_mod",
    )(cc, w_ada, b_ada.reshape(depth, 1, n))


def _norm_kernel(x_ref, g_ref, sc_ref, sh_ref, h_ref):
    h_ref[...] = _norm_mod(x_ref[...], g_ref[...], sc_ref[...], sh_ref[...]).astype(BF16)


def _norm(x3, g, scale, shift, tm):
    G, T, d = x3.shape
    mod_spec = pl.BlockSpec((None, 1, d), lambda gi, i: (gi, 0, 0))
    return pl.pallas_call(
        _norm_kernel,
        out_shape=jax.ShapeDtypeStruct((G, T, d), BF16),
        grid=(G, T // tm),
        in_specs=[
            pl.BlockSpec((None, tm, d), lambda gi, i: (gi, i, 0)),
            pl.BlockSpec((1, d), lambda gi, i: (0, 0)),
            mod_spec, mod_spec,
        ],
        out_specs=pl.BlockSpec((None, tm, d), lambda gi, i: (gi, i, 0)),
        compiler_params=_params("parallel", "parallel"),
        name="norm_mod",
    )(x3, g, scale, shift)


def _cast_weights_once(w_ref, wbf_s):
    @pl.when(pl.program_id(1) == 0)
    def _():
        wbf_s[...] = w_ref[...].astype(BF16)


def _lin_kernel(h_ref, w_ref, o_ref, wbf_s):
    _cast_weights_once(w_ref, wbf_s)
    o_ref[...] = jnp.dot(h_ref[...], wbf_s[...], preferred_element_type=F32)


def _act_kernel(h_ref, w_ref, o_ref, wbf_s):
    _cast_weights_once(w_ref, wbf_s)
    acc = jnp.dot(h_ref[...], wbf_s[...], preferred_element_type=F32)
    s = jax.nn.sigmoid(acc)
    o_ref[...] = jnp.where(pl.program_id(0) < 3, acc * s, s).astype(BF16)


def _in_proj(kind, h, w_in, layer, n_blocks, tm):
    T, d = h.shape
    if kind == "lin":
        body, dtype = _lin_kernel, F32
        blk = lambda j: (BLK_XB - BLK_XA) * j
    else:
        body, dtype = _act_kernel, BF16
        blk = lambda j: jnp.where(j < 3, 1 + 2 * j + j // 2, j + BLK_GL - 3)
    return pl.pallas_call(
        body,
        out_shape=jax.ShapeDtypeStruct((T, n_blocks * BRANCH_W), dtype),
        grid=(n_blocks, T // tm),
        in_specs=[
            pl.BlockSpec((tm, d), lambda j, i: (i, 0)),
            pl.BlockSpec((None, d, BRANCH_W), lambda j, i: (layer, 0, blk(j))),
        ],
        out_specs=pl.BlockSpec((tm, BRANCH_W), lambda j, i: (i, j)),
        scratch_shapes=[pltpu.VMEM((d, BRANCH_W), BF16)],
        compiler_params=_params("parallel", "arbitrary"),
        name="in_proj_" + kind,
    )(h, w_in)


def _rglru_kernel(xa_ref, cw_ref, cb_ref, wri_ref, bri_ref, lam_ref, h0_ref,
                  ya_ref, hfin_ref, ua_s, hf_s, af_s, hb_s, ab_s, *, L, JB):
    S = L // SUBLANES
    xa = xa_ref[...]
    row = lax.broadcasted_iota(jnp.int32, (L, HEAD_DIM), 0)
    ua = jnp.zeros((L, HEAD_DIM), F32) + cb_ref[...]
    for k in range(CONV_A_WIDTH):
        o = k - CONV_A_PAD_LEFT
        sh = xa if o == 0 else pltpu.roll(xa, (-o) % L, axis=0)
        valid = (row + o >= 0) & (row + o < L)
        ua = ua + cw_ref[k:k + 1, :] * jnp.where(valid, sh, 0.0)
    for s in range(SUBLANES):
        for jb in range(S // SUBLANES):
            t0 = s * S + jb * SUBLANES
            ua_s[pl.ds(jb * SUBLANES * SUBLANES + s, SUBLANES, stride=SUBLANES), :] = ua[t0:t0 + SUBLANES]

    nlam = -lam_ref[...]
    sp = jnp.maximum(nlam, 0.0) + jnp.log1p(jnp.exp(-jnp.abs(nlam)))
    rate = -LRU_C * sp

    def seg_rows(j):
        return (pl.ds(j, SUBLANES, stride=S), slice(None))

    def coeffs(j0, d):
        u = ua_s[pl.ds(pl.multiple_of(j0 * SUBLANES, JB * SUBLANES), JB * SUBLANES), :]
        c0 = 2 * d * HEAD_DIM
        g = (jnp.dot(u.astype(BF16), wri_ref[:, c0:c0 + 2 * HEAD_DIM], preferred_element_type=F32)
             + bri_ref[:, c0:c0 + 2 * HEAD_DIM])
        r = jax.nn.sigmoid(g[:, :HEAD_DIM])
        ig = jax.nn.sigmoid(g[:, HEAD_DIM:])
        log_a = r * rate[d:d + 1, :]
        a = jnp.exp(log_a)
        mult = jnp.sqrt(-jnp.tanh(log_a) * (a * a + 1.0))
        return a, mult * ig * u

    def block(i, carry):
        hf, af, hb, ab = carry
        j0 = i * JB
        a, b = coeffs(j0, 0)
        for v in range(JB):
            av = a[v * SUBLANES:(v + 1) * SUBLANES]
            hf = av * hf + b[v * SUBLANES:(v + 1) * SUBLANES]
            af = av * af
            hf_s[j0 + v] = hf
            af_s[j0 + v] = af
        j0 = S - (i + 1) * JB
        a, b = coeffs(j0, 1)
        for v in reversed(range(JB)):
            av = a[v * SUBLANES:(v + 1) * SUBLANES]
            hb = av * hb + b[v * SUBLANES:(v + 1) * SUBLANES]
            ab = av * ab
            hb_s[j0 + v] = hb
            ab_s[j0 + v] = ab
        return hf, af, hb, ab

    zero = jnp.zeros((SUBLANES, HEAD_DIM), F32)
    one = jnp.ones((SUBLANES, HEAD_DIM), F32)
    hf, af, hb, ab = lax.fori_loop(0, S // JB, block, (zero, one, zero, one))

    cf = [h0_ref[0:1, :]]
    for s in range(SUBLANES):
        cf.append(hf[s:s + 1] + af[s:s + 1] * cf[-1])
    cb = [h0_ref[1:2, :]]
    for s in reversed(range(SUBLANES)):
        cb.append(hb[s:s + 1] + ab[s:s + 1] * cb[-1])
    cb = cb[::-1]
    hfin_ref[0:1, :] = cf[SUBLANES]
    hfin_ref[1:2, :] = cb[0]
    enter_f = jnp.concatenate(cf[:SUBLANES], axis=0)
    enter_b = jnp.concatenate(cb[1:], axis=0)

    def finish(i, _):
        for v in range(JB):
            j = i * JB + v
            ya_ref[seg_rows(j)] = (hf_s[j] + af_s[j] * enter_f) + (hb_s[j] + ab_s[j] * enter_b)
        return 0

    lax.fori_loop(0, S // JB, finish, 0)


def _rglru(lin3, conv_w, conv_b, wri, bri, lam, h0):
    B, L, _ = lin3.shape
    S = L // SUBLANES
    JB = min(S, 32)
    col = pl.BlockSpec((None, L, HEAD_DIM), lambda b, h: (b, 0, h))
    scr = pltpu.VMEM((S, SUBLANES, HEAD_DIM), F32)
    return pl.pallas_call(
        functools.partial(_rglru_kernel, L=L, JB=JB),
        out_shape=(jax.ShapeDtypeStruct((B, L, BRANCH_W), F32),
                   jax.ShapeDtypeStruct((B, 2, BRANCH_W), F32)),
        grid=(B, LRU_HEADS),
        in_specs=[
            col,
            pl.BlockSpec((CONV_A_WIDTH, HEAD_DIM), lambda b, h: (0, h)),
            pl.BlockSpec((1, HEAD_DIM), lambda b, h: (0, h)),
            pl.BlockSpec((None, HEAD_DIM, 4 * HEAD_DIM), lambda b, h: (h, 0, 0)),
            pl.BlockSpec((None, 1, 4 * HEAD_DIM), lambda b, h: (h, 0, 0)),
            pl.BlockSpec((2, HEAD_DIM), lambda b, h: (0, h)),
            pl.BlockSpec((None, 2, HEAD_DIM), lambda b, h: (b, 0, h)),
        ],
        out_specs=(col, pl.BlockSpec((None, 2, HEAD_DIM), lambda b, h: (b, 0, h))),
        scratch_shapes=[pltpu.VMEM((L, HEAD_DIM), F32)] + [scr] * 4,
        compiler_params=_params("parallel", "parallel"),
        name="rglru",
    )(lin3, conv_w, conv_b, wri, bri, lam, h0)


_CONVC_HALO = 16
GLU_CHUNK = 256
CONV_RB = 64


def _glu_conv_kernel(tap_ref, h_ref, wv_ref, wg_ref, cw_ref, cb_ref, o_ref, buf_s, *, tm, nt):
    H = _CONVC_HALO
    i = pl.program_id(1)

    @pl.when(i == 0)
    def _():
        buf_s[...] = jnp.zeros(buf_s.shape, F32)

    h = h_ref[...]
    rb = min(CONV_RB, tm)
    first = H - CONV_C_PAD_LEFT
    base = pl.multiple_of(tap_ref[0], HEAD_DIM)
    for c0 in range(0, BRANCH_W, GLU_CHUNK):
        cols = slice(c0, c0 + GLU_CHUNK)
        v = jnp.dot(h, wv_ref[:, cols], preferred_element_type=F32)
        g = jnp.dot(h, wg_ref[:, cols], preferred_element_type=F32)
        y = v * jax.nn.sigmoid(g)
        for l0 in range(c0, c0 + GLU_CHUNK, HEAD_DIM):
            lb = l0 // HEAD_DIM
            lanes = slice(l0, l0 + HEAD_DIM)
            yl = y[:, l0 - c0:l0 - c0 + HEAD_DIM]
            buf_s[lb, H + tm:2 * H + tm, :] = jnp.where(i < nt, yl[0:H], 0.0)
            for r0 in range(0, tm, rb):
                acc = jnp.zeros((rb, HEAD_DIM), F32) + cb_ref[:, lanes]
                for k in range(CONV_C_WIDTH):
                    acc = acc + cw_ref[k:k + 1, lanes] * buf_s[lb, pl.ds(base + (r0 + first + k), rb), :]
                o_ref[r0:r0 + rb, lanes] = acc
            buf_s[lb, 0:H, :] = buf_s[lb, tm:tm + H, :]
            buf_s[lb, H:H + tm, :] = yl


def _glu_conv(h3, w_cv, w_cg, layer, conv_w, conv_b, tm):
    B, L, d = h3.shape
    nt = L // tm
    w_spec = pl.BlockSpec((None, d, BRANCH_W), lambda b, i, t: (layer, 0, 0))
    taps = jnp.zeros((1,), jnp.int32)
    return pl.pallas_call(
        functools.partial(_glu_conv_kernel, tm=tm, nt=nt),
        out_shape=jax.ShapeDtypeStruct((B, L, BRANCH_W), F32),
        grid_spec=pltpu.PrefetchScalarGridSpec(
            num_scalar_prefetch=1,
            grid=(B, nt + 1),
            in_specs=[
                pl.BlockSpec((None, tm, d), lambda b, i, t: (b, jnp.minimum(i, nt - 1), 0)),
                w_spec, w_spec,
                pl.BlockSpec((CONV_C_WIDTH, BRANCH_W), lambda b, i, t: (0, 0)),
                pl.BlockSpec((1, BRANCH_W), lambda b, i, t: (0, 0)),
            ],
            out_specs=pl.BlockSpec((None, tm, BRANCH_W), lambda b, i, t: (b, jnp.maximum(i - 1, 0), 0)),
            scratch_shapes=[pltpu.VMEM((BRANCH_W // HEAD_DIM, tm + 2 * _CONVC_HALO, HEAD_DIM), F32)],
        ),
        compiler_params=_params("parallel", "arbitrary"),
        name="glu_conv",
    )(taps, h3, w_cv, w_cg, conv_w, conv_b)


def _pool_tail_kernel(xb_ref, szb_ref, yc_ref, szc_ref, pw_ref, ps_ref, lng_ref, lnb_ref, pwc_ref, pwcb_ref,
                      gb_ref, gc_ref, *, tm, n_row):
    pos = lax.broadcasted_iota(jnp.int32, (tm, POOL_GROUP), 0) % n_row
    for k, w in enumerate(POOL_WINDOWS):
        c0 = k * POOL_GROUP
        u = xb_ref[:, c0:c0 + POOL_GROUP]
        tot = u
        for o in range(-(w // 2), w // 2):
            if o == 0:
                continue
            sh = pltpu.roll(u, (-o) % tm, axis=0)
            tot = tot + jnp.where((pos + o >= 0) & (pos + o < n_row), sh, 0.0)
        cnt = jnp.minimum(pos + w // 2, n_row) - jnp.maximum(pos - w // 2, 0)
        p = tot / cnt.astype(F32) - u
        y = jnp.dot(p.astype(BF16), pw_ref[k], preferred_element_type=F32) * ps_ref[:, c0:c0 + POOL_GROUP]
        gb_ref[:, c0:c0 + POOL_GROUP] = (y * szb_ref[:, c0:c0 + POOL_GROUP].astype(F32)).astype(BF16)

    yc = yc_ref[...]
    mu = jnp.mean(yc, axis=-1, keepdims=True)
    dev = yc - mu
    var = jnp.mean(dev * dev, axis=-1, keepdims=True)
    t = _silu(dev * lax.rsqrt(var + EPS) * lng_ref[...] + lnb_ref[...])
    y = jnp.dot(t.astype(BF16), pwc_ref[...], preferred_element_type=F32) + pwcb_ref[...]
    gc_ref[...] = (y * szc_ref[...].astype(F32)).astype(BF16)


def _pool_tail(lin, act, yconv, layer, pool_w, pool_scale, ln_g, ln_b, pwc_w, pwc_b, tm, n_row):
    T = lin.shape[0]
    col = lambda blk: pl.BlockSpec((tm, BRANCH_W), lambda i: (i, blk))
    vec = pl.BlockSpec((1, BRANCH_W), lambda i: (0, 0))
    return pl.pallas_call(
        functools.partial(_pool_tail_kernel, tm=tm, n_row=n_row),
        out_shape=(jax.ShapeDtypeStruct((T, BRANCH_W), BF16), jax.ShapeDtypeStruct((T, BRANCH_W), BF16)),
        grid=(T // tm,),
        in_specs=[
            col(1), col(1), col(0), col(2),
            pl.BlockSpec((None,) + pool_w.shape[1:], lambda i: (layer, 0, 0, 0)),
            vec, vec, vec,
            pl.BlockSpec((None, BRANCH_W, BRANCH_W), lambda i: (layer, 0, 0)),
            vec,
        ],
        out_specs=(col(0), col(0)),
        compiler_params=_params("parallel"),
        name="pool_tail",
    )(lin, act, yconv, act, pool_w, pool_scale, ln_g, ln_b, pwc_w, pwc_b)


def _merge_kernel(ya_ref, sza_ref, gb_ref, gc_ref, w_ref, gt0_ref, gt1_ref, gt2_ref, o_ref):
    ga = (ya_ref[...] * sza_ref[...].astype(F32)).astype(BF16)
    acc = None
    for k, (g, gt_ref) in enumerate(((ga, gt0_ref), (gb_ref[...], gt1_ref), (gc_ref[...], gt2_ref))):
        t = gt_ref[...].astype(F32) * jnp.dot(g, w_ref[k], preferred_element_type=F32)
        acc = t if acc is None else acc + t
    o_ref[...] = acc.astype(BF16)


def _merge(ya, gb, gc, w_bout, layer, act, tm, tn):
    T = gb.shape[0]
    g_spec = pl.BlockSpec((tm, BRANCH_W), lambda i, j: (i, 0))
    gate = lambda k: pl.BlockSpec((tm, tn), lambda i, j, o=(3 * BRANCH_W + k * D_MODEL) // tn: (i, o + j))
    return pl.pallas_call(
        _merge_kernel,
        out_shape=jax.ShapeDtypeStruct((T, D_MODEL), BF16),
        grid=(T // tm, D_MODEL // tn),
        in_specs=[
            g_spec, g_spec, g_spec, g_spec,
            pl.BlockSpec((None, N_BRANCH, BRANCH_W, tn), lambda i, j: (layer, 0, 0, j)),
            gate(0), gate(1), gate(2),
        ],
        out_specs=pl.BlockSpec((tm, tn), lambda i, j: (i, j)),
        compiler_params=_params("parallel", "parallel"),
        name="merge",
    )(ya, act, gb, gc, w_bout, act, act, act)


def _out_mid_kernel(acc_ref, x_ref, w_ref, b_ref, gate_ref, g_ref, sc_ref, sh_ref, xo_ref, h_ref):
    out = jnp.dot(acc_ref[...], w_ref[...], preferred_element_type=F32) + b_ref[...]
    xn = x_ref[...] + gate_ref[...] * out
    xo_ref[...] = xn
    h_ref[...] = _norm_mod(xn, g_ref[...], sc_ref[...], sh_ref[...]).astype(BF16)


def _out_last_kernel(acc_ref, x_ref, w_ref, b_ref, gate_ref, g_ref, o_ref):
    out = jnp.dot(acc_ref[...], w_ref[...], preferred_element_type=F32) + b_ref[...]
    xn = x_ref[...] + gate_ref[...] * out
    o_ref[...] = xn * lax.rsqrt(jnp.mean(xn * xn, axis=-1, keepdims=True) + EPS) * g_ref[...]


def _out_proj(acc3, x3, w_out, layer, b_out, gate, g_next, scale_next, shift_next, tm, last):
    G, T, d = x3.shape
    tile = pl.BlockSpec((None, tm, d), lambda gi, i: (gi, i, 0))
    mod = pl.BlockSpec((None, 1, d), lambda gi, i: (gi, 0, 0))
    vec = pl.BlockSpec((1, d), lambda gi, i: (0, 0))
    wspec = pl.BlockSpec((None, d, d), lambda gi, i: (layer, 0, 0))
    if last:
        return pl.pallas_call(
            _out_last_kernel,
            out_shape=jax.ShapeDtypeStruct((G, T, d), F32),
            grid=(G, T // tm),
            in_specs=[tile, tile, wspec, vec, mod, vec],
            out_specs=tile,
            compiler_params=_params("parallel", "parallel"),
            name="out_last",
        )(acc3, x3, w_out, b_out, gate, g_next)
    return pl.pallas_call(
        _out_mid_kernel,
        out_shape=(jax.ShapeDtypeStruct((G, T, d), F32), jax.ShapeDtypeStruct((G, T, d), BF16)),
        grid=(G, T // tm),
        in_specs=[tile, tile, wspec, vec, mod, vec, mod, mod],
        out_specs=(tile, tile),
        compiler_params=_params("parallel", "parallel"),
        name="out_mid",
    )(acc3, x3, w_out, b_out, gate, g_next, scale_next, shift_next)


def _mix(h, wts, lw, layer, B, L, n_row, h0, state_only, tm_in, tm_tail, tm_merge):
    T = B * L
    lin = _in_proj("lin", h, wts["w_in"], layer, 1 if state_only else 2, tm_in)
    ya, hfin = _rglru(lin.reshape(B, L, -1), lw["conv_a_w"], lw["conv_a_b"], wts["wri"][layer],
                      wts["bri"][layer], lw["lam"], h0)
    if state_only:
        return None, hfin
    act = _in_proj("act", h, wts["w_in"], layer, N_ACT_BLOCKS, tm_in)
    yconv = _glu_conv(h.reshape(B, L, -1), wts["w_cv"], wts["w_cg"], layer, lw["convc_w"], lw["convc_b"],
                      min(L, 512))
    gb, gc = _pool_tail(lin, act, yconv.reshape(T, BRANCH_W), layer, wts["pool_w"], lw["pool_scale"],
                        lw["lnc_g"], lw["lnc_b"], wts["pwc_w"], lw["pwc_b"], tm_tail, n_row)
    acc = _merge(ya.reshape(T, BRANCH_W), gb, gc, wts["w_bout"], layer, act, tm_merge, 512)
    return acc, hfin


def kernel(x, c, ctx, c_ctx, norm_g, w_ada, b_ada, w_in, conv_a_w, conv_a_b, lru_wr, lru_br, lru_wi, lru_bi,
           lru_lambda, pool_w, pool_scale, convc_w, convc_b, lnc_g, lnc_b, pwc_w, pwc_b, w_bout, w_out, b_out,
           final_g):
    B, L, d = x.shape
    CL = ctx.shape[1]
    depth = w_in.shape[0]
    TC = B * CL

    rows = -(-(B + 1) // SUBLANES) * SUBLANES
    cc = jnp.zeros((rows, d), F32).at[:B].set(c).at[B].set(c_ctx)
    mod = _ada(cc, w_ada, b_ada)

    def mods(l):
        sh, sc, gt = (mod[l, :, k * d:(k + 1) * d] for k in range(3))
        lat = tuple(v[:B].reshape(B, 1, d) for v in (sh, sc, gt))
        cx = tuple(v[B:B + 1].reshape(1, 1, d) for v in (sh, sc, gt))
        return lat, cx

    wts = {
        "w_in": w_in,
        "w_cv": w_in[:, :, BLK_CV * BRANCH_W:(BLK_CV + 1) * BRANCH_W].astype(BF16),
        "w_cg": w_in[:, :, BLK_CG * BRANCH_W:(BLK_CG + 1) * BRANCH_W].astype(BF16),
        "wri": jnp.concatenate([lru_wr[:, 0], lru_wi[:, 0], lru_wr[:, 1], lru_wi[:, 1]], axis=-1).astype(BF16),
        "bri": jnp.concatenate([lru_br[:, 0], lru_bi[:, 0], lru_br[:, 1], lru_bi[:, 1]],
                               axis=-1).reshape(depth, LRU_HEADS, 1, 4 * HEAD_DIM),
        "pool_w": pool_w.astype(BF16),
        "pwc_w": pwc_w.astype(BF16),
        "w_bout": w_bout.astype(BF16),
        "w_out": w_out.astype(BF16),
    }

    def layer_params(l):
        return {
            "conv_a_w": conv_a_w[l], "conv_a_b": conv_a_b[l].reshape(1, -1),
            "lam": lru_lambda[l],
            "pool_scale": pool_scale[l].reshape(1, -1),
            "convc_w": convc_w[l], "convc_b": convc_b[l].reshape(1, -1),
            "lnc_g": lnc_g[l].reshape(1, -1), "lnc_b": lnc_b[l].reshape(1, -1),
            "pwc_b": pwc_b[l].reshape(1, -1),
            "b_out": b_out[l].reshape(1, -1),
        }

    tm_lat = min(L, 512)
    tm_ctx = min(TC, 512)
    tm_in = min(B * L, 1024)
    x3 = x
    xc3 = ctx.reshape(1, TC, d)
    (sh, sc, _), (shc, scc, _) = mods(0)
    h = _norm(x3, norm_g[0].reshape(1, d), sc, sh, tm_lat)
    hc = _norm(xc3, norm_g[0].reshape(1, d), scc, shc, tm_ctx)
    zeros_h0 = jnp.zeros((B, 2, BRANCH_W), F32)

    for l in range(depth):
        last = l == depth - 1
        lw = layer_params(l)
        (_, _, gate), (_, _, gate_c) = mods(l)
        acc_c, hfin = _mix(hc.reshape(TC, d), wts, lw, l, B, CL, CL, zeros_h0, last, tm_ctx, CL, tm_ctx)
        acc, _ = _mix(h.reshape(B * L, d), wts, lw, l, B, L, GRID_W, hfin, False, tm_in, tm_lat,
                      min(B * L, 1024))
        if last:
            return _out_proj(acc.reshape(B, L, d), x3, wts["w_out"], l, lw["b_out"], gate,
                             final_g.reshape(1, d), None, None, tm_lat, True)
        (sh, sc, _), (shc, scc, _) = mods(l + 1)
        g_next = norm_g[l + 1].reshape(1, d)
        x3, h = _out_proj(acc.reshape(B, L, d), x3, wts["w_out"], l, lw["b_out"], gate, g_next, sc, sh,
                          tm_lat, False)
        xc3, hc = _out_proj(acc_c.reshape(1, TC, d), xc3, wts["w_out"], l, lw["b_out"], gate_c, g_next, scc,
                            shc, tm_ctx, False)
```

```python
import functools

import jax
import jax.numpy as jnp
import numpy as np
from jax import lax
from jax.experimental import pallas as pl
from jax.experimental.pallas import tpu as pltpu

F32 = jnp.float32
BF16 = jnp.bfloat16

D_MODEL = 2048
BRANCH_W = 1024
LRU_HEADS = 8
HEAD_DIM = BRANCH_W // LRU_HEADS
LRU_C = 8.0
CONV_A_WIDTH = 4
CONV_A_PAD_LEFT = 2
GRID_W = 64
POOL_WINDOWS = (2, 4, 8, 16)
POOL_GROUP = BRANCH_W // len(POOL_WINDOWS)
CONV_C_WIDTH = 31
CONV_C_PAD_LEFT = (CONV_C_WIDTH - 1) // 2
N_BRANCH = 3
EPS = 1e-6
BLK_XA, BLK_ZA, BLK_XB, BLK_ZB, BLK_CV, BLK_CG, BLK_ZC, BLK_GL = range(8)
N_ACT_BLOCKS = 3 + N_BRANCH * D_MODEL // BRANCH_W

SUBLANES = 8
V7X_VMEM_LIMIT_BYTES = 56 * 1024 * 1024


def _params(*sem):
    return pltpu.CompilerParams(dimension_semantics=sem, vmem_limit_bytes=V7X_VMEM_LIMIT_BYTES)


def _silu(v):
    return v * jax.nn.sigmoid(v)


def _norm_mod(x, g, scale, shift):
    y = x * lax.rsqrt(jnp.mean(x * x, axis=-1, keepdims=True) + EPS) * g
    return y * (1.0 + scale) + shift


def _ada_kernel(c_ref, w_ref, b_ref, o_ref):
    s = _silu(c_ref[...])
    o_ref[...] = jnp.dot(s.astype(BF16), w_ref[...].astype(BF16), preferred_element_type=F32) + b_ref[...]


def _ada(cc, w_ada, b_ada, tn=512):
    depth, d, n = w_ada.shape
    rows = cc.shape[0]
    return pl.pallas_call(
        _ada_kernel,
        out_shape=jax.ShapeDtypeStruct((depth, rows, n), F32),
        grid=(depth, n // tn),
        in_specs=[
            pl.BlockSpec((rows, d), lambda l, j: (0, 0)),
            pl.BlockSpec((None, d, tn), lambda l, j: (l, 0, j)),
            pl.BlockSpec((None, 1, tn), lambda l, j: (l, 0, j)),
        ],
        out_specs=pl.BlockSpec((None, rows, tn), lambda l, j: (l, 0, j)),
        compiler_params=_params("parallel", "parallel"),
        name="ada_mod",
    )(cc, w_ada, b_ada.reshape(depth, 1, n))


def _norm_kernel(x_ref, g_ref, sc_ref, sh_ref, h_ref):
    h_ref[...] = _norm_mod(x_ref[...], g_ref[...], sc_ref[...], sh_ref[...]).astype(BF16)


def _norm(x3, g, scale, shift, tm):
    G, T, d = x3.shape
    mod_spec = pl.BlockSpec((None, 1, d), lambda gi, i: (gi, 0, 0))
    return pl.pallas_call(
        _norm_kernel,
        out_shape=jax.ShapeDtypeStruct((G, T, d), BF16),
        grid=(G, T // tm),
        in_specs=[
            pl.BlockSpec((None, tm, d), lambda gi, i: (gi, i, 0)),
            pl.BlockSpec((1, d), lambda gi, i: (0, 0)),
            mod_spec, mod_spec,
        ],
        out_specs=pl.BlockSpec((None, tm, d), lambda gi, i: (gi, i, 0)),
        compiler_params=_params("parallel", "parallel"),
        name="norm_mod",
    )(x3, g, scale, shift)


def _cast_weights_once(w_ref, wbf_s):
    @pl.when(pl.program_id(1) == 0)
    def _():
        wbf_s[...] = w_ref[...].astype(BF16)


def _lin_kernel(h_ref, w_ref, o_ref, wbf_s):
    _cast_weights_once(w_ref, wbf_s)
    o_ref[...] = jnp.dot(h_ref[...], wbf_s[...], preferred_element_type=F32)


def _act_kernel(h_ref, w_ref, o_ref, wbf_s):
    _cast_weights_once(w_ref, wbf_s)
    acc = jnp.dot(h_ref[...], wbf_s[...], preferred_element_type=F32)
    s = jax.nn.sigmoid(acc)
    o_ref[...] = jnp.where(pl.program_id(0) < 3, acc * s, s).astype(BF16)


def _in_proj(kind, h, w_in, layer, n_blocks, tm):
    T, d = h.shape
    if kind == "lin":
        body, dtype = _lin_kernel, F32
        blk = lambda j: (BLK_XB - BLK_XA) * j
    else:
        body, dtype = _act_kernel, BF16
        blk = lambda j: jnp.where(j < 3, 1 + 2 * j + j // 2, j + BLK_GL - 3)
    return pl.pallas_call(
        body,
        out_shape=jax.ShapeDtypeStruct((T, n_blocks * BRANCH_W), dtype),
        grid=(n_blocks, T // tm),
        in_specs=[
            pl.BlockSpec((tm, d), lambda j, i: (i, 0)),
            pl.BlockSpec((None, d, BRANCH_W), lambda j, i: (layer, 0, blk(j))),
        ],
        out_specs=pl.BlockSpec((tm, BRANCH_W), lambda j, i: (i, j)),
        scratch_shapes=[pltpu.VMEM((d, BRANCH_W), BF16)],
        compiler_params=_params("parallel", "arbitrary"),
        name="in_proj_" + kind,
    )(h, w_in)


def _rglru_kernel(xa_ref, cw_ref, cb_ref, wri_ref, bri_ref, lam_ref, h0_ref, perm_ref,
                  ya_ref, hfin_ref, ua_s, hf_s, af_s, hb_s, ab_s, *, L, JB):
    S = L // SUBLANES
    xa = xa_ref[...]
    row = lax.broadcasted_iota(jnp.int32, (L, HEAD_DIM), 0)
    ua = jnp.zeros((L, HEAD_DIM), F32) + cb_ref[...]
    for k in range(CONV_A_WIDTH):
        o = k - CONV_A_PAD_LEFT
        sh = xa if o == 0 else pltpu.roll(xa, (-o) % L, axis=0)
        valid = (row + o >= 0) & (row + o < L)
        ua = ua + cw_ref[k:k + 1, :] * jnp.where(valid, sh, 0.0)
    for s in range(SUBLANES):
        for jb in range(S // SUBLANES):
            t0 = s * S + jb * SUBLANES
            ua_s[pl.ds(jb * SUBLANES * SUBLANES + s, SUBLANES, stride=SUBLANES), :] = ua[t0:t0 + SUBLANES]

    nlam = -lam_ref[...]
    sp = jnp.maximum(nlam, 0.0) + jnp.log1p(jnp.exp(-jnp.abs(nlam)))
    rate = -LRU_C * sp

    def coeffs(j0, d):
        u = ua_s[pl.ds(pl.multiple_of(j0 * SUBLANES, JB * SUBLANES), JB * SUBLANES), :]
        c0 = 2 * d * HEAD_DIM
        g = (jnp.dot(u.astype(BF16), wri_ref[:, c0:c0 + 2 * HEAD_DIM], preferred_element_type=F32)
             + bri_ref[:, c0:c0 + 2 * HEAD_DIM])
        r = jax.nn.sigmoid(g[:, :HEAD_DIM])
        ig = jax.nn.sigmoid(g[:, HEAD_DIM:])
        log_a = r * rate[d:d + 1, :]
        a = jnp.exp(log_a)
        mult = jnp.sqrt(-jnp.tanh(log_a) * (a * a + 1.0))
        return a, mult * ig * u

    def block(i, carry):
        hf, af, hb, ab = carry
        j0 = i * JB
        a, b = coeffs(j0, 0)
        for v in range(JB):
            av = a[v * SUBLANES:(v + 1) * SUBLANES]
            hf = av * hf + b[v * SUBLANES:(v + 1) * SUBLANES]
            af = av * af
            hf_s[j0 + v] = hf
            af_s[j0 + v] = af
        j0 = S - (i + 1) * JB
        a, b = coeffs(j0, 1)
        for v in reversed(range(JB)):
            av = a[v * SUBLANES:(v + 1) * SUBLANES]
            hb = av * hb + b[v * SUBLANES:(v + 1) * SUBLANES]
            ab = av * ab
            hb_s[j0 + v] = hb
            ab_s[j0 + v] = ab
        return hf, af, hb, ab

    zero = jnp.zeros((SUBLANES, HEAD_DIM), F32)
    one = jnp.ones((SUBLANES, HEAD_DIM), F32)
    hf, af, hb, ab = lax.fori_loop(0, S // JB, block, (zero, one, zero, one))

    cf = [h0_ref[0:1, :]]
    for s in range(SUBLANES):
        cf.append(hf[s:s + 1] + af[s:s + 1] * cf[-1])
    cb = [h0_ref[1:2, :]]
    for s in reversed(range(SUBLANES)):
        cb.append(hb[s:s + 1] + ab[s:s + 1] * cb[-1])
    cb = cb[::-1]
    hfin_ref[0:1, :] = cf[SUBLANES]
    hfin_ref[1:2, :] = cb[0]
    enter_f = jnp.concatenate(cf[:SUBLANES], axis=0)
    enter_b = jnp.concatenate(cb[1:], axis=0)

    JT = perm_ref.shape[0] // SUBLANES
    for blk in range(S // JT):
        steps = slice(blk * JT, (blk + 1) * JT)
        ysm = (hf_s[steps] + af_s[steps] * enter_f) + (hb_s[steps] + ab_s[steps] * enter_b)
        ynat = jnp.dot(perm_ref[...], ysm.reshape(JT * SUBLANES, HEAD_DIM).astype(BF16),
                       preferred_element_type=F32)
        for s in range(SUBLANES):
            ya_ref[s * S + blk * JT:s * S + (blk + 1) * JT, :] = ynat[s * JT:(s + 1) * JT].astype(BF16)


def _rglru(lin3, conv_w, conv_b, wri, bri, lam, h0):
    B, L, _ = lin3.shape
    S = L // SUBLANES
    JB = min(S, 32)
    JT = min(S, 32)
    perm = np.zeros((SUBLANES * JT, SUBLANES * JT), np.float32)
    for s in range(SUBLANES):
        for jj in range(JT):
            perm[s * JT + jj, jj * SUBLANES + s] = 1.0
    perm = jnp.asarray(perm, BF16)
    col = pl.BlockSpec((None, L, HEAD_DIM), lambda b, h: (b, 0, h))
    scr = pltpu.VMEM((S, SUBLANES, HEAD_DIM), F32)
    return pl.pallas_call(
        functools.partial(_rglru_kernel, L=L, JB=JB),
        out_shape=(jax.ShapeDtypeStruct((B, L, BRANCH_W), BF16),
                   jax.ShapeDtypeStruct((B, 2, BRANCH_W), F32)),
        grid=(B, LRU_HEADS),
        in_specs=[
            col,
            pl.BlockSpec((CONV_A_WIDTH, HEAD_DIM), lambda b, h: (0, h)),
            pl.BlockSpec((1, HEAD_DIM), lambda b, h: (0, h)),
            pl.BlockSpec((None, HEAD_DIM, 4 * HEAD_DIM), lambda b, h: (h, 0, 0)),
            pl.BlockSpec((None, 1, 4 * HEAD_DIM), lambda b, h: (h, 0, 0)),
            pl.BlockSpec((2, HEAD_DIM), lambda b, h: (0, h)),
            pl.BlockSpec((None, 2, HEAD_DIM), lambda b, h: (b, 0, h)),
            pl.BlockSpec(perm.shape, lambda b, h: (0, 0)),
        ],
        out_specs=(col, pl.BlockSpec((None, 2, HEAD_DIM), lambda b, h: (b, 0, h))),
        scratch_shapes=[pltpu.VMEM((L, HEAD_DIM), F32)] + [scr] * 4,
        compiler_params=_params("parallel", "parallel"),
        name="rglru",
    )(lin3, conv_w, conv_b, wri, bri, lam, h0, perm)


_CONVC_HALO = 16
GLU_CHUNK = 256
CONV_RB = 64


def _glu_conv_kernel(tap_ref, h_ref, wv_ref, wg_ref, cw_ref, cb_ref, o_ref, buf_s, *, tm, nt):
    H = _CONVC_HALO
    i = pl.program_id(1)

    @pl.when(i == 0)
    def _():
        buf_s[...] = jnp.zeros(buf_s.shape, F32)

    h = h_ref[...]
    rb = min(CONV_RB, tm)
    first = H - CONV_C_PAD_LEFT
    base = pl.multiple_of(tap_ref[0], HEAD_DIM)
    for c0 in range(0, BRANCH_W, GLU_CHUNK):
        cols = slice(c0, c0 + GLU_CHUNK)
        v = jnp.dot(h, wv_ref[:, cols], preferred_element_type=F32)
        g = jnp.dot(h, wg_ref[:, cols], preferred_element_type=F32)
        y = v * jax.nn.sigmoid(g)
        for l0 in range(c0, c0 + GLU_CHUNK, HEAD_DIM):
            lb = l0 // HEAD_DIM
            lanes = slice(l0, l0 + HEAD_DIM)
            yl = y[:, l0 - c0:l0 - c0 + HEAD_DIM]
            buf_s[lb, H + tm:2 * H + tm, :] = jnp.where(i < nt, yl[0:H], 0.0)
            for r0 in range(0, tm, rb):
                acc = jnp.zeros((rb, HEAD_DIM), F32) + cb_ref[:, lanes]
                for k in range(CONV_C_WIDTH):
                    acc = acc + cw_ref[k:k + 1, lanes] * buf_s[lb, pl.ds(base + (r0 + first + k), rb), :]
                o_ref[r0:r0 + rb, lanes] = acc
            buf_s[lb, 0:H, :] = buf_s[lb, tm:tm + H, :]
            buf_s[lb, H:H + tm, :] = yl


def _cast_kernel(w_ref, o_ref):
    o_ref[...] = w_ref[...].astype(BF16)


def _glu_weights(w_in, tk=512):
    depth, d, _ = w_in.shape
    return pl.pallas_call(
        _cast_kernel,
        out_shape=jax.ShapeDtypeStruct((depth, 2, d, BRANCH_W), BF16),
        grid=(depth, 2, d // tk),
        in_specs=[pl.BlockSpec((None, tk, BRANCH_W), lambda l, c, k: (l, k, BLK_CV + c))],
        out_specs=pl.BlockSpec((None, None, tk, BRANCH_W), lambda l, c, k: (l, c, k, 0)),
        compiler_params=_params("parallel", "parallel", "parallel"),
        name="glu_weights",
    )(w_in)


def _glu_conv(h3, w_glu, layer, conv_w, conv_b, tm):
    B, L, d = h3.shape
    nt = L // tm
    w_spec = lambda c: pl.BlockSpec((None, None, d, BRANCH_W), lambda b, i, t: (layer, c, 0, 0))
    taps = jnp.zeros((1,), jnp.int32)
    return pl.pallas_call(
        functools.partial(_glu_conv_kernel, tm=tm, nt=nt),
        out_shape=jax.ShapeDtypeStruct((B, L, BRANCH_W), F32),
        grid_spec=pltpu.PrefetchScalarGridSpec(
            num_scalar_prefetch=1,
            grid=(B, nt + 1),
            in_specs=[
                pl.BlockSpec((None, tm, d), lambda b, i, t: (b, jnp.minimum(i, nt - 1), 0)),
                w_spec(0), w_spec(1),
                pl.BlockSpec((CONV_C_WIDTH, BRANCH_W), lambda b, i, t: (0, 0)),
                pl.BlockSpec((1, BRANCH_W), lambda b, i, t: (0, 0)),
            ],
            out_specs=pl.BlockSpec((None, tm, BRANCH_W), lambda b, i, t: (b, jnp.maximum(i - 1, 0), 0)),
            scratch_shapes=[pltpu.VMEM((BRANCH_W // HEAD_DIM, tm + 2 * _CONVC_HALO, HEAD_DIM), F32)],
        ),
        compiler_params=_params("parallel", "arbitrary"),
        name="glu_conv",
    )(taps, h3, w_glu, w_glu, conv_w, conv_b)


def _pool_tail_kernel(xb_ref, szb_ref, yc_ref, szc_ref, pw_ref, ps_ref, lng_ref, lnb_ref, pwc_ref, pwcb_ref,
                      gb_ref, gc_ref, *, tm, n_row):
    pos = lax.broadcasted_iota(jnp.int32, (tm, POOL_GROUP), 0) % n_row
    for k, w in enumerate(POOL_WINDOWS):
        c0 = k * POOL_GROUP
        u = xb_ref[:, c0:c0 + POOL_GROUP]
        tot = u
        for o in range(-(w // 2), w // 2):
            if o == 0:
                continue
            sh = pltpu.roll(u, (-o) % tm, axis=0)
            tot = tot + jnp.where((pos + o >= 0) & (pos + o < n_row), sh, 0.0)
        cnt = jnp.minimum(pos + w // 2, n_row) - jnp.maximum(pos - w // 2, 0)
        p = tot / cnt.astype(F32) - u
        y = jnp.dot(p.astype(BF16), pw_ref[k], preferred_element_type=F32) * ps_ref[:, c0:c0 + POOL_GROUP]
        gb_ref[:, c0:c0 + POOL_GROUP] = (y * szb_ref[:, c0:c0 + POOL_GROUP].astype(F32)).astype(BF16)

    yc = yc_ref[...]
    mu = jnp.mean(yc, axis=-1, keepdims=True)
    dev = yc - mu
    var = jnp.mean(dev * dev, axis=-1, keepdims=True)
    t = _silu(dev * lax.rsqrt(var + EPS) * lng_ref[...] + lnb_ref[...])
    y = jnp.dot(t.astype(BF16), pwc_ref[...], preferred_element_type=F32) + pwcb_ref[...]
    gc_ref[...] = (y * szc_ref[...].astype(F32)).astype(BF16)


def _pool_tail(lin, act, yconv, layer, pool_w, pool_scale, ln_g, ln_b, pwc_w, pwc_b, tm, n_row):
    T = lin.shape[0]
    col = lambda blk: pl.BlockSpec((tm, BRANCH_W), lambda i: (i, blk))
    vec = pl.BlockSpec((1, BRANCH_W), lambda i: (0, 0))
    return pl.pallas_call(
        functools.partial(_pool_tail_kernel, tm=tm, n_row=n_row),
        out_shape=(jax.ShapeDtypeStruct((T, BRANCH_W), BF16), jax.ShapeDtypeStruct((T, BRANCH_W), BF16)),
        grid=(T // tm,),
        in_specs=[
            col(1), col(1), col(0), col(2),
            pl.BlockSpec((None,) + pool_w.shape[1:], lambda i: (layer, 0, 0, 0)),
            vec, vec, vec,
            pl.BlockSpec((None, BRANCH_W, BRANCH_W), lambda i: (layer, 0, 0)),
            vec,
        ],
        out_specs=(col(0), col(0)),
        compiler_params=_params("parallel"),
        name="pool_tail",
    )(lin, act, yconv, act, pool_w, pool_scale, ln_g, ln_b, pwc_w, pwc_b)


def _merge_kernel(ya_ref, sza_ref, gb_ref, gc_ref, w_ref, gt0_ref, gt1_ref, gt2_ref, o_ref):
    ga = ya_ref[...] * sza_ref[...]
    acc = None
    for k, (g, gt_ref) in enumerate(((ga, gt0_ref), (gb_ref[...], gt1_ref), (gc_ref[...], gt2_ref))):
        t = gt_ref[...].astype(F32) * jnp.dot(g, w_ref[k], preferred_element_type=F32)
        acc = t if acc is None else acc + t
    o_ref[...] = acc.astype(BF16)


def _merge(ya, gb, gc, w_bout, layer, act, tm, tn):
    T = gb.shape[0]
    g_spec = pl.BlockSpec((tm, BRANCH_W), lambda i, j: (i, 0))
    gate = lambda k: pl.BlockSpec((tm, tn), lambda i, j, o=(3 * BRANCH_W + k * D_MODEL) // tn: (i, o + j))
    return pl.pallas_call(
        _merge_kernel,
        out_shape=jax.ShapeDtypeStruct((T, D_MODEL), BF16),
        grid=(T // tm, D_MODEL // tn),
        in_specs=[
            g_spec, g_spec, g_spec, g_spec,
            pl.BlockSpec((None, N_BRANCH, BRANCH_W, tn), lambda i, j: (layer, 0, 0, j)),
            gate(0), gate(1), gate(2),
        ],
        out_specs=pl.BlockSpec((tm, tn), lambda i, j: (i, j)),
        compiler_params=_params("parallel", "parallel"),
        name="merge",
    )(ya, act, gb, gc, w_bout, act, act, act)


def _out_mid_kernel(acc_ref, x_ref, w_ref, b_ref, gate_ref, g_ref, sc_ref, sh_ref, xo_ref, h_ref):
    out = jnp.dot(acc_ref[...], w_ref[...], preferred_element_type=F32) + b_ref[...]
    xn = x_ref[...] + gate_ref[...] * out
    xo_ref[...] = xn
    h_ref[...] = _norm_mod(xn, g_ref[...], sc_ref[...], sh_ref[...]).astype(BF16)


def _out_last_kernel(acc_ref, x_ref, w_ref, b_ref, gate_ref, g_ref, o_ref):
    out = jnp.dot(acc_ref[...], w_ref[...], preferred_element_type=F32) + b_ref[...]
    xn = x_ref[...] + gate_ref[...] * out
    o_ref[...] = xn * lax.rsqrt(jnp.mean(xn * xn, axis=-1, keepdims=True) + EPS) * g_ref[...]


def _out_proj(acc3, x3, w_out, layer, b_out, gate, g_next, scale_next, shift_next, tm, last):
    G, T, d = x3.shape
    tile = pl.BlockSpec((None, tm, d), lambda gi, i: (gi, i, 0))
    mod = pl.BlockSpec((None, 1, d), lambda gi, i: (gi, 0, 0))
    vec = pl.BlockSpec((1, d), lambda gi, i: (0, 0))
    wspec = pl.BlockSpec((None, d, d), lambda gi, i: (layer, 0, 0))
    if last:
        return pl.pallas_call(
            _out_last_kernel,
            out_shape=jax.ShapeDtypeStruct((G, T, d), F32),
            grid=(G, T // tm),
            in_specs=[tile, tile, wspec, vec, mod, vec],
            out_specs=tile,
            compiler_params=_params("parallel", "parallel"),
            name="out_last",
        )(acc3, x3, w_out, b_out, gate, g_next)
    return pl.pallas_call(
        _out_mid_kernel,
        out_shape=(jax.ShapeDtypeStruct((G, T, d), F32), jax.ShapeDtypeStruct((G, T, d), BF16)),
        grid=(G, T // tm),
        in_specs=[tile, tile, wspec, vec, mod, vec, mod, mod],
        out_specs=(tile, tile),
        compiler_params=_params("parallel", "parallel"),
        name="out_mid",
    )(acc3, x3, w_out, b_out, gate, g_next, scale_next, shift_next)


def _mix(h, wts, lw, layer, B, L, n_row, h0, state_only, tm_in, tm_tail, tm_merge):
    T = B * L
    lin = _in_proj("lin", h, wts["w_in"], layer, 1 if state_only else 2, tm_in)
    ya, hfin = _rglru(lin.reshape(B, L, -1), lw["conv_a_w"], lw["conv_a_b"], wts["wri"][layer],
                      wts["bri"][layer], lw["lam"], h0)
    if state_only:
        return None, hfin
    act = _in_proj("act", h, wts["w_in"], layer, N_ACT_BLOCKS, tm_in)
    yconv = _glu_conv(h.reshape(B, L, -1), wts["w_glu"], layer, lw["convc_w"], lw["convc_b"], min(L, 512))
    gb, gc = _pool_tail(lin, act, yconv.reshape(T, BRANCH_W), layer, wts["pool_w"], lw["pool_scale"],
                        lw["lnc_g"], lw["lnc_b"], wts["pwc_w"], lw["pwc_b"], tm_tail, n_row)
    acc = _merge(ya.reshape(T, BRANCH_W), gb, gc, wts["w_bout"], layer, act, tm_merge, 512)
    return acc, hfin


def kernel(x, c, ctx, c_ctx, norm_g, w_ada, b_ada, w_in, conv_a_w, conv_a_b, lru_wr, lru_br, lru_wi, lru_bi,
           lru_lambda, pool_w, pool_scale, convc_w, convc_b, lnc_g, lnc_b, pwc_w, pwc_b, w_bout, w_out, b_out,
           final_g):
    B, L, d = x.shape
    CL = ctx.shape[1]
    depth = w_in.shape[0]
    TC = B * CL

    rows = -(-(B + 1) // SUBLANES) * SUBLANES
    cc = jnp.zeros((rows, d), F32).at[:B].set(c).at[B].set(c_ctx)
    mod = _ada(cc, w_ada, b_ada)

    def mods(l):
        sh, sc, gt = (mod[l, :, k * d:(k + 1) * d] for k in range(3))
        lat = tuple(v[:B].reshape(B, 1, d) for v in (sh, sc, gt))
        cx = tuple(v[B:B + 1].reshape(1, 1, d) for v in (sh, sc, gt))
        return lat, cx

    wts = {
        "w_in": w_in,
        "w_glu": _glu_weights(w_in),
        "wri": jnp.concatenate([lru_wr[:, 0], lru_wi[:, 0], lru_wr[:, 1], lru_wi[:, 1]], axis=-1).astype(BF16),
        "bri": jnp.concatenate([lru_br[:, 0], lru_bi[:, 0], lru_br[:, 1], lru_bi[:, 1]],
                               axis=-1).reshape(depth, LRU_HEADS, 1, 4 * HEAD_DIM),
        "pool_w": pool_w.astype(BF16),
        "pwc_w": pwc_w.astype(BF16),
        "w_bout": w_bout.astype(BF16),
        "w_out": w_out.astype(BF16),
    }

    def layer_params(l):
        return {
            "conv_a_w": conv_a_w[l], "conv_a_b": conv_a_b[l].reshape(1, -1),
            "lam": lru_lambda[l],
            "pool_scale": pool_scale[l].reshape(1, -1),
            "convc_w": convc_w[l], "convc_b": convc_b[l].reshape(1, -1),
            "lnc_g": lnc_g[l].reshape(1, -1), "lnc_b": lnc_b[l].reshape(1, -1),
            "pwc_b": pwc_b[l].reshape(1, -1),
            "b_out": b_out[l].reshape(1, -1),
        }

    tm_lat = min(L, 512)
    tm_ctx = min(TC, 512)
    tm_in = min(B * L, 1024)
    x3 = x
    xc3 = ctx.reshape(1, TC, d)
    (sh, sc, _), (shc, scc, _) = mods(0)
    h = _norm(x3, norm_g[0].reshape(1, d), sc, sh, tm_lat)
    hc = _norm(xc3, norm_g[0].reshape(1, d), scc, shc, tm_ctx)
    zeros_h0 = jnp.zeros((B, 2, BRANCH_W), F32)

    for l in range(depth):
        last = l == depth - 1
        lw = layer_params(l)
        (_, _, gate), (_, _, gate_c) = mods(l)
        acc_c, hfin = _mix(hc.reshape(TC, d), wts, lw, l, B, CL, CL, zeros_h0, last, tm_ctx, CL, tm_ctx)
        acc, _ = _mix(h.reshape(B * L, d), wts, lw, l, B, L, GRID_W, hfin, False, tm_in, tm_lat,
                      min(B * L, 1024))
        if last:
            return _out_proj(acc.reshape(B, L, d), x3, wts["w_out"], l, lw["b_out"], gate,
                             final_g.reshape(1, d), None, None, tm_lat, True)
        (sh, sc, _), (shc, scc, _) = mods(l + 1)
        g_next = norm_g[l + 1].reshape(1, d)
        x3, h = _out_proj(acc.reshape(B, L, d), x3, wts["w_out"], l, lw["b_out"], gate, g_next, sc, sh,
                          tm_lat, False)
        xc3, hc = _out_proj(acc_c.reshape(1, TC, d), xc3, wts["w_out"], l, lw["b_out"], gate_c, g_next, scc,
                            shc, tm_ctx, False)
```

```python
import functools

import jax
import jax.numpy as jnp
import numpy as np
from jax import lax
from jax.experimental import pallas as pl
from jax.experimental.pallas import tpu as pltpu

F32 = jnp.float32
BF16 = jnp.bfloat16

D_MODEL = 2048
BRANCH_W = 1024
LRU_HEADS = 8
HEAD_DIM = BRANCH_W // LRU_HEADS
LRU_C = 8.0
CONV_A_WIDTH = 4
CONV_A_PAD_LEFT = 2
GRID_W = 64
POOL_WINDOWS = (2, 4, 8, 16)
POOL_GROUP = BRANCH_W // len(POOL_WINDOWS)
CONV_C_WIDTH = 31
CONV_C_PAD_LEFT = (CONV_C_WIDTH - 1) // 2
N_BRANCH = 3
EPS = 1e-6
BLK_XA, BLK_ZA, BLK_XB, BLK_ZB, BLK_CV, BLK_CG, BLK_ZC, BLK_GL = range(8)
N_ACT_BLOCKS = 3 + N_BRANCH * D_MODEL // BRANCH_W

SUBLANES = 8
V7X_VMEM_LIMIT_BYTES = 56 * 1024 * 1024


def _params(*sem):
    return pltpu.CompilerParams(dimension_semantics=sem, vmem_limit_bytes=V7X_VMEM_LIMIT_BYTES)


def _silu(v):
    return v * jax.nn.sigmoid(v)


def _norm_mod(x, g, scale, shift):
    y = x * lax.rsqrt(jnp.mean(x * x, axis=-1, keepdims=True) + EPS) * g
    return y * (1.0 + scale) + shift


def _ada_kernel(c_ref, w_ref, b_ref, o_ref):
    s = _silu(c_ref[...])
    o_ref[...] = jnp.dot(s.astype(BF16), w_ref[...].astype(BF16), preferred_element_type=F32) + b_ref[...]


def _ada(cc, w_ada, b_ada, tn=512):
    depth, d, n = w_ada.shape
    rows = cc.shape[0]
    return pl.pallas_call(
        _ada_kernel,
        out_shape=jax.ShapeDtypeStruct((depth, rows, n), F32),
        grid=(depth, n // tn),
        in_specs=[
            pl.BlockSpec((rows, d), lambda l, j: (0, 0)),
            pl.BlockSpec((None, d, tn), lambda l, j: (l, 0, j)),
            pl.BlockSpec((None, 1, tn), lambda l, j: (l, 0, j)),
        ],
        out_specs=pl.BlockSpec((None, rows, tn), lambda l, j: (l, 0, j)),
        compiler_params=_params("parallel", "parallel"),
        name="ada_mod",
    )(cc, w_ada, b_ada.reshape(depth, 1, n))


def _norm_kernel(x_ref, g_ref, sc_ref, sh_ref, h_ref):
    h_ref[...] = _norm_mod(x_ref[...], g_ref[...], sc_ref[...], sh_ref[...]).astype(BF16)


def _norm(x3, g, scale, shift, tm):
    G, T, d = x3.shape
    mod_spec = pl.BlockSpec((None, 1, d), lambda gi, i: (gi, 0, 0))
    return pl.pallas_call(
        _norm_kernel,
        out_shape=jax.ShapeDtypeStruct((G, T, d), BF16),
        grid=(G, T // tm),
        in_specs=[
            pl.BlockSpec((None, tm, d), lambda gi, i: (gi, i, 0)),
            pl.BlockSpec((1, d), lambda gi, i: (0, 0)),
            mod_spec, mod_spec,
        ],
        out_specs=pl.BlockSpec((None, tm, d), lambda gi, i: (gi, i, 0)),
        compiler_params=_params("parallel", "parallel"),
        name="norm_mod",
    )(x3, g, scale, shift)


def _cast_weights_once(w_ref, wbf_s):
    @pl.when(pl.program_id(1) == 0)
    def _():
        wbf_s[...] = w_ref[...].astype(BF16)


def _lin_kernel(h_ref, w_ref, o_ref, wbf_s):
    _cast_weights_once(w_ref, wbf_s)
    o_ref[...] = jnp.dot(h_ref[...], wbf_s[...], preferred_element_type=F32)


def _act_kernel(h_ref, w_ref, o_ref, wbf_s):
    _cast_weights_once(w_ref, wbf_s)
    acc = jnp.dot(h_ref[...], wbf_s[...], preferred_element_type=F32)
    s = jax.nn.sigmoid(acc)
    o_ref[...] = jnp.where(pl.program_id(0) < 3, acc * s, s).astype(BF16)


def _in_proj(kind, h, w_in, layer, n_blocks, tm):
    T, d = h.shape
    if kind == "lin":
        body, dtype = _lin_kernel, F32
        blk = lambda j: (BLK_XB - BLK_XA) * j
    else:
        body, dtype = _act_kernel, BF16
        blk = lambda j: jnp.where(j < 3, 1 + 2 * j + j // 2, j + BLK_GL - 3)
    return pl.pallas_call(
        body,
        out_shape=jax.ShapeDtypeStruct((T, n_blocks * BRANCH_W), dtype),
        grid=(n_blocks, T // tm),
        in_specs=[
            pl.BlockSpec((tm, d), lambda j, i: (i, 0)),
            pl.BlockSpec((None, d, BRANCH_W), lambda j, i: (layer, 0, blk(j))),
        ],
        out_specs=pl.BlockSpec((tm, BRANCH_W), lambda j, i: (i, j)),
        scratch_shapes=[pltpu.VMEM((d, BRANCH_W), BF16)],
        compiler_params=_params("parallel", "arbitrary"),
        name="in_proj_" + kind,
    )(h, w_in)


def _rglru_kernel(xa_ref, cw_ref, cb_ref, wri_ref, bri_ref, lam_ref, h0_ref, perm_ref,
                  ya_ref, hfin_ref, ua_s, hf_s, af_s, hb_s, ab_s, *, L, JB):
    S = L // SUBLANES
    xa = xa_ref[...]
    row = lax.broadcasted_iota(jnp.int32, (L, HEAD_DIM), 0)
    ua = jnp.zeros((L, HEAD_DIM), F32) + cb_ref[...]
    for k in range(CONV_A_WIDTH):
        o = k - CONV_A_PAD_LEFT
        sh = xa if o == 0 else pltpu.roll(xa, (-o) % L, axis=0)
        valid = (row + o >= 0) & (row + o < L)
        ua = ua + cw_ref[k:k + 1, :] * jnp.where(valid, sh, 0.0)
    for s in range(SUBLANES):
        for jb in range(S // SUBLANES):
            t0 = s * S + jb * SUBLANES
            ua_s[pl.ds(jb * SUBLANES * SUBLANES + s, SUBLANES, stride=SUBLANES), :] = ua[t0:t0 + SUBLANES]

    nlam = -lam_ref[...]
    sp = jnp.maximum(nlam, 0.0) + jnp.log1p(jnp.exp(-jnp.abs(nlam)))
    rate = -LRU_C * sp

    def coeffs(j0, d):
        u = ua_s[pl.ds(pl.multiple_of(j0 * SUBLANES, JB * SUBLANES), JB * SUBLANES), :]
        c0 = 2 * d * HEAD_DIM
        g = (jnp.dot(u.astype(BF16), wri_ref[:, c0:c0 + 2 * HEAD_DIM], preferred_element_type=F32)
             + bri_ref[:, c0:c0 + 2 * HEAD_DIM])
        r = jax.nn.sigmoid(g[:, :HEAD_DIM])
        ig = jax.nn.sigmoid(g[:, HEAD_DIM:])
        log_a = r * rate[d:d + 1, :]
        a = jnp.exp(log_a)
        mult = jnp.sqrt(-jnp.tanh(log_a) * (a * a + 1.0))
        return a, mult * ig * u

    def block(i, carry):
        hf, af, hb, ab = carry
        j0 = i * JB
        a, b = coeffs(j0, 0)
        for v in range(JB):
            av = a[v * SUBLANES:(v + 1) * SUBLANES]
            hf = av * hf + b[v * SUBLANES:(v + 1) * SUBLANES]
            af = av * af
            hf_s[j0 + v] = hf
            af_s[j0 + v] = af
        j0 = S - (i + 1) * JB
        a, b = coeffs(j0, 1)
        for v in reversed(range(JB)):
            av = a[v * SUBLANES:(v + 1) * SUBLANES]
            hb = av * hb + b[v * SUBLANES:(v + 1) * SUBLANES]
            ab = av * ab
            hb_s[j0 + v] = hb
            ab_s[j0 + v] = ab
        return hf, af, hb, ab

    zero = jnp.zeros((SUBLANES, HEAD_DIM), F32)
    one = jnp.ones((SUBLANES, HEAD_DIM), F32)
    hf, af, hb, ab = lax.fori_loop(0, S // JB, block, (zero, one, zero, one))

    cf = [h0_ref[0:1, :]]
    for s in range(SUBLANES):
        cf.append(hf[s:s + 1] + af[s:s + 1] * cf[-1])
    cb = [h0_ref[1:2, :]]
    for s in reversed(range(SUBLANES)):
        cb.append(hb[s:s + 1] + ab[s:s + 1] * cb[-1])
    cb = cb[::-1]
    hfin_ref[0:1, :] = cf[SUBLANES]
    hfin_ref[1:2, :] = cb[0]
    enter_f = jnp.concatenate(cf[:SUBLANES], axis=0)
    enter_b = jnp.concatenate(cb[1:], axis=0)

    JT = perm_ref.shape[0] // SUBLANES
    for blk in range(S // JT):
        steps = slice(blk * JT, (blk + 1) * JT)
        ysm = (hf_s[steps] + af_s[steps] * enter_f) + (hb_s[steps] + ab_s[steps] * enter_b)
        ynat = jnp.dot(perm_ref[...], ysm.reshape(JT * SUBLANES, HEAD_DIM).astype(BF16),
                       preferred_element_type=F32)
        for s in range(SUBLANES):
            ya_ref[s * S + blk * JT:s * S + (blk + 1) * JT, :] = ynat[s * JT:(s + 1) * JT].astype(BF16)


def _rglru(lin3, conv_w, conv_b, wri, bri, lam, h0):
    B, L, _ = lin3.shape
    S = L // SUBLANES
    JB = min(S, 128)
    JT = min(S, 32)
    perm = np.zeros((SUBLANES * JT, SUBLANES * JT), np.float32)
    for s in range(SUBLANES):
        for jj in range(JT):
            perm[s * JT + jj, jj * SUBLANES + s] = 1.0
    perm = jnp.asarray(perm, BF16)
    col = pl.BlockSpec((None, L, HEAD_DIM), lambda b, h: (b, 0, h))
    scr = pltpu.VMEM((S, SUBLANES, HEAD_DIM), F32)
    return pl.pallas_call(
        functools.partial(_rglru_kernel, L=L, JB=JB),
        out_shape=(jax.ShapeDtypeStruct((B, L, BRANCH_W), BF16),
                   jax.ShapeDtypeStruct((B, 2, BRANCH_W), F32)),
        grid=(B, LRU_HEADS),
        in_specs=[
            col,
            pl.BlockSpec((CONV_A_WIDTH, HEAD_DIM), lambda b, h: (0, h)),
            pl.BlockSpec((1, HEAD_DIM), lambda b, h: (0, h)),
            pl.BlockSpec((None, HEAD_DIM, 4 * HEAD_DIM), lambda b, h: (h, 0, 0)),
            pl.BlockSpec((None, 1, 4 * HEAD_DIM), lambda b, h: (h, 0, 0)),
            pl.BlockSpec((2, HEAD_DIM), lambda b, h: (0, h)),
            pl.BlockSpec((None, 2, HEAD_DIM), lambda b, h: (b, 0, h)),
            pl.BlockSpec(perm.shape, lambda b, h: (0, 0)),
        ],
        out_specs=(col, pl.BlockSpec((None, 2, HEAD_DIM), lambda b, h: (b, 0, h))),
        scratch_shapes=[pltpu.VMEM((L, HEAD_DIM), F32)] + [scr] * 4,
        compiler_params=_params("parallel", "parallel"),
        name="rglru",
    )(lin3, conv_w, conv_b, wri, bri, lam, h0, perm)


_CONVC_HALO = 16
GLU_CHUNK = 256
CONV_RB = 32


def _glu_conv_kernel(tap_ref, h_ref, wv_ref, wg_ref, cw_ref, cb_ref, o_ref, buf_s, *, tm, nt):
    H = _CONVC_HALO
    i = pl.program_id(1)

    @pl.when(i == 0)
    def _():
        buf_s[...] = jnp.zeros(buf_s.shape, F32)

    h = h_ref[...]
    rb = min(CONV_RB, tm)
    first = H - CONV_C_PAD_LEFT
    base = pl.multiple_of(tap_ref[0], HEAD_DIM)
    for c0 in range(0, BRANCH_W, GLU_CHUNK):
        cols = slice(c0, c0 + GLU_CHUNK)
        v = jnp.dot(h, wv_ref[:, cols], preferred_element_type=F32)
        g = jnp.dot(h, wg_ref[:, cols], preferred_element_type=F32)
        y = v * jax.nn.sigmoid(g)
        for l0 in range(c0, c0 + GLU_CHUNK, HEAD_DIM):
            lb = l0 // HEAD_DIM
            lanes = slice(l0, l0 + HEAD_DIM)
            yl = y[:, l0 - c0:l0 - c0 + HEAD_DIM]
            buf_s[lb, H + tm:2 * H + tm, :] = jnp.where(i < nt, yl[0:H], 0.0)
            for r0 in range(0, tm, rb):
                acc = jnp.zeros((rb, HEAD_DIM), F32) + cb_ref[:, lanes]
                for k in range(CONV_C_WIDTH):
                    acc = acc + cw_ref[k:k + 1, lanes] * buf_s[lb, pl.ds(base + (r0 + first + k), rb), :]
                o_ref[r0:r0 + rb, lanes] = acc
            buf_s[lb, 0:H, :] = buf_s[lb, tm:tm + H, :]
            buf_s[lb, H:H + tm, :] = yl


def _cast_kernel(w_ref, o_ref):
    o_ref[...] = w_ref[...].astype(BF16)


def _glu_weights(w_in, tk=512):
    depth, d, _ = w_in.shape
    return pl.pallas_call(
        _cast_kernel,
        out_shape=jax.ShapeDtypeStruct((depth, 2, d, BRANCH_W), BF16),
        grid=(depth, 2, d // tk),
        in_specs=[pl.BlockSpec((None, tk, BRANCH_W), lambda l, c, k: (l, k, BLK_CV + c))],
        out_specs=pl.BlockSpec((None, None, tk, BRANCH_W), lambda l, c, k: (l, c, k, 0)),
        compiler_params=_params("parallel", "parallel", "parallel"),
        name="glu_weights",
    )(w_in)


def _glu_conv(h3, w_glu, layer, conv_w, conv_b, tm):
    B, L, d = h3.shape
    nt = L // tm
    w_spec = lambda c: pl.BlockSpec((None, None, d, BRANCH_W), lambda b, i, t: (layer, c, 0, 0))
    taps = jnp.zeros((1,), jnp.int32)
    return pl.pallas_call(
        functools.partial(_glu_conv_kernel, tm=tm, nt=nt),
        out_shape=jax.ShapeDtypeStruct((B, L, BRANCH_W), F32),
        grid_spec=pltpu.PrefetchScalarGridSpec(
            num_scalar_prefetch=1,
            grid=(B, nt + 1),
            in_specs=[
                pl.BlockSpec((None, tm, d), lambda b, i, t: (b, jnp.minimum(i, nt - 1), 0)),
                w_spec(0), w_spec(1),
                pl.BlockSpec((CONV_C_WIDTH, BRANCH_W), lambda b, i, t: (0, 0)),
                pl.BlockSpec((1, BRANCH_W), lambda b, i, t: (0, 0)),
            ],
            out_specs=pl.BlockSpec((None, tm, BRANCH_W), lambda b, i, t: (b, jnp.maximum(i - 1, 0), 0)),
            scratch_shapes=[pltpu.VMEM((BRANCH_W // HEAD_DIM, tm + 2 * _CONVC_HALO, HEAD_DIM), F32)],
        ),
        compiler_params=_params("parallel", "arbitrary"),
        name="glu_conv",
    )(taps, h3, w_glu, w_glu, conv_w, conv_b)


def _pool_tail_kernel(xb_ref, szb_ref, yc_ref, szc_ref, pw_ref, ps_ref, lng_ref, lnb_ref, pwc_ref, pwcb_ref,
                      gb_ref, gc_ref, *, tm, n_row):
    pos = lax.broadcasted_iota(jnp.int32, (tm, POOL_GROUP), 0) % n_row
    for k, w in enumerate(POOL_WINDOWS):
        c0 = k * POOL_GROUP
        u = xb_ref[:, c0:c0 + POOL_GROUP]
        tot = u
        for o in range(-(w // 2), w // 2):
            if o == 0:
                continue
            sh = pltpu.roll(u, (-o) % tm, axis=0)
            tot = tot + jnp.where((pos + o >= 0) & (pos + o < n_row), sh, 0.0)
        cnt = jnp.minimum(pos + w // 2, n_row) - jnp.maximum(pos - w // 2, 0)
        p = tot / cnt.astype(F32) - u
        y = jnp.dot(p.astype(BF16), pw_ref[k], preferred_element_type=F32) * ps_ref[:, c0:c0 + POOL_GROUP]
        gb_ref[:, c0:c0 + POOL_GROUP] = (y * szb_ref[:, c0:c0 + POOL_GROUP].astype(F32)).astype(BF16)

    yc = yc_ref[...]
    mu = jnp.mean(yc, axis=-1, keepdims=True)
    dev = yc - mu
    var = jnp.mean(dev * dev, axis=-1, keepdims=True)
    t = _silu(dev * lax.rsqrt(var + EPS) * lng_ref[...] + lnb_ref[...])
    y = jnp.dot(t.astype(BF16), pwc_ref[...], preferred_element_type=F32) + pwcb_ref[...]
    gc_ref[...] = (y * szc_ref[...].astype(F32)).astype(BF16)


def _pool_tail(lin, act, yconv, layer, pool_w, pool_scale, ln_g, ln_b, pwc_w, pwc_b, tm, n_row):
    T = lin.shape[0]
    col = lambda blk: pl.BlockSpec((tm, BRANCH_W), lambda i: (i, blk))
    vec = pl.BlockSpec((1, BRANCH_W), lambda i: (0, 0))
    return pl.pallas_call(
        functools.partial(_pool_tail_kernel, tm=tm, n_row=n_row),
        out_shape=(jax.ShapeDtypeStruct((T, BRANCH_W), BF16), jax.ShapeDtypeStruct((T, BRANCH_W), BF16)),
        grid=(T // tm,),
        in_specs=[
            col(1), col(1), col(0), col(2),
            pl.BlockSpec((None,) + pool_w.shape[1:], lambda i: (layer, 0, 0, 0)),
            vec, vec, vec,
            pl.BlockSpec((None, BRANCH_W, BRANCH_W), lambda i: (layer, 0, 0)),
            vec,
        ],
        out_specs=(col(0), col(0)),
        compiler_params=_params("parallel"),
        name="pool_tail",
    )(lin, act, yconv, act, pool_w, pool_scale, ln_g, ln_b, pwc_w, pwc_b)


def _merge_kernel(ya_ref, sza_ref, gb_ref, gc_ref, w_ref, gt0_ref, gt1_ref, gt2_ref, o_ref):
    ga = ya_ref[...] * sza_ref[...]
    acc = None
    for k, (g, gt_ref) in enumerate(((ga, gt0_ref), (gb_ref[...], gt1_ref), (gc_ref[...], gt2_ref))):
        t = gt_ref[...].astype(F32) * jnp.dot(g, w_ref[k], preferred_element_type=F32)
        acc = t if acc is None else acc + t
    o_ref[...] = acc.astype(BF16)


def _merge(ya, gb, gc, w_bout, layer, act, tm, tn):
    T = gb.shape[0]
    g_spec = pl.BlockSpec((tm, BRANCH_W), lambda i, j: (i, 0))
    gate = lambda k: pl.BlockSpec((tm, tn), lambda i, j, o=(3 * BRANCH_W + k * D_MODEL) // tn: (i, o + j))
    return pl.pallas_call(
        _merge_kernel,
        out_shape=jax.ShapeDtypeStruct((T, D_MODEL), BF16),
        grid=(T // tm, D_MODEL // tn),
        in_specs=[
            g_spec, g_spec, g_spec, g_spec,
            pl.BlockSpec((None, N_BRANCH, BRANCH_W, tn), lambda i, j: (layer, 0, 0, j)),
            gate(0), gate(1), gate(2),
        ],
        out_specs=pl.BlockSpec((tm, tn), lambda i, j: (i, j)),
        compiler_params=_params("parallel", "parallel"),
        name="merge",
    )(ya, act, gb, gc, w_bout, act, act, act)


def _out_mid_kernel(acc_ref, x_ref, w_ref, b_ref, gate_ref, g_ref, sc_ref, sh_ref, xo_ref, h_ref):
    out = jnp.dot(acc_ref[...], w_ref[...], preferred_element_type=F32) + b_ref[...]
    xn = x_ref[...] + gate_ref[...] * out
    xo_ref[...] = xn
    h_ref[...] = _norm_mod(xn, g_ref[...], sc_ref[...], sh_ref[...]).astype(BF16)


def _out_last_kernel(acc_ref, x_ref, w_ref, b_ref, gate_ref, g_ref, o_ref):
    out = jnp.dot(acc_ref[...], w_ref[...], preferred_element_type=F32) + b_ref[...]
    xn = x_ref[...] + gate_ref[...] * out
    o_ref[...] = xn * lax.rsqrt(jnp.mean(xn * xn, axis=-1, keepdims=True) + EPS) * g_ref[...]


def _out_proj(acc3, x3, w_out, layer, b_out, gate, g_next, scale_next, shift_next, tm, last):
    G, T, d = x3.shape
    tile = pl.BlockSpec((None, tm, d), lambda gi, i: (gi, i, 0))
    mod = pl.BlockSpec((None, 1, d), lambda gi, i: (gi, 0, 0))
    vec = pl.BlockSpec((1, d), lambda gi, i: (0, 0))
    wspec = pl.BlockSpec((None, d, d), lambda gi, i: (layer, 0, 0))
    if last:
        return pl.pallas_call(
            _out_last_kernel,
            out_shape=jax.ShapeDtypeStruct((G, T, d), F32),
            grid=(G, T // tm),
            in_specs=[tile, tile, wspec, vec, mod, vec],
            out_specs=tile,
            compiler_params=_params("parallel", "parallel"),
            name="out_last",
        )(acc3, x3, w_out, b_out, gate, g_next)
    return pl.pallas_call(
        _out_mid_kernel,
        out_shape=(jax.ShapeDtypeStruct((G, T, d), F32), jax.ShapeDtypeStruct((G, T, d), BF16)),
        grid=(G, T // tm),
        in_specs=[tile, tile, wspec, vec, mod, vec, mod, mod],
        out_specs=(tile, tile),
        compiler_params=_params("parallel", "parallel"),
        name="out_mid",
    )(acc3, x3, w_out, b_out, gate, g_next, scale_next, shift_next)


def _mix(h, wts, lw, layer, B, L, n_row, h0, state_only, tm_in, tm_tail, tm_merge):
    T = B * L
    lin = _in_proj("lin", h, wts["w_in"], layer, 1 if state_only else 2, tm_in)
    ya, hfin = _rglru(lin.reshape(B, L, -1), lw["conv_a_w"], lw["conv_a_b"], wts["wri"][layer],
                      wts["bri"][layer], lw["lam"], h0)
    if state_only:
        return None, hfin
    act = _in_proj("act", h, wts["w_in"], layer, N_ACT_BLOCKS, tm_in)
    yconv = _glu_conv(h.reshape(B, L, -1), wts["w_glu"], layer, lw["convc_w"], lw["convc_b"], min(L, 512))
    gb, gc = _pool_tail(lin, act, yconv.reshape(T, BRANCH_W), layer, wts["pool_w"], lw["pool_scale"],
                        lw["lnc_g"], lw["lnc_b"], wts["pwc_w"], lw["pwc_b"], tm_tail, n_row)
    acc = _merge(ya.reshape(T, BRANCH_W), gb, gc, wts["w_bout"], layer, act, tm_merge, 512)
    return acc, hfin


def kernel(x, c, ctx, c_ctx, norm_g, w_ada, b_ada, w_in, conv_a_w, conv_a_b, lru_wr, lru_br, lru_wi, lru_bi,
           lru_lambda, pool_w, pool_scale, convc_w, convc_b, lnc_g, lnc_b, pwc_w, pwc_b, w_bout, w_out, b_out,
           final_g):
    B, L, d = x.shape
    CL = ctx.shape[1]
    depth = w_in.shape[0]
    TC = B * CL

    rows = -(-(B + 1) // SUBLANES) * SUBLANES
    cc = jnp.zeros((rows, d), F32).at[:B].set(c).at[B].set(c_ctx)
    mod = _ada(cc, w_ada, b_ada)

    def mods(l):
        sh, sc, gt = (mod[l, :, k * d:(k + 1) * d] for k in range(3))
        lat = tuple(v[:B].reshape(B, 1, d) for v in (sh, sc, gt))
        cx = tuple(v[B:B + 1].reshape(1, 1, d) for v in (sh, sc, gt))
        return lat, cx

    wts = {
        "w_in": w_in,
        "w_glu": _glu_weights(w_in),
        "wri": jnp.concatenate([lru_wr[:, 0], lru_wi[:, 0], lru_wr[:, 1], lru_wi[:, 1]], axis=-1).astype(BF16),
        "bri": jnp.concatenate([lru_br[:, 0], lru_bi[:, 0], lru_br[:, 1], lru_bi[:, 1]],
                               axis=-1).reshape(depth, LRU_HEADS, 1, 4 * HEAD_DIM),
        "pool_w": pool_w.astype(BF16),
        "pwc_w": pwc_w.astype(BF16),
        "w_bout": w_bout.astype(BF16),
        "w_out": w_out.astype(BF16),
    }

    def layer_params(l):
        return {
            "conv_a_w": conv_a_w[l], "conv_a_b": conv_a_b[l].reshape(1, -1),
            "lam": lru_lambda[l],
            "pool_scale": pool_scale[l].reshape(1, -1),
            "convc_w": convc_w[l], "convc_b": convc_b[l].reshape(1, -1),
            "lnc_g": lnc_g[l].reshape(1, -1), "lnc_b": lnc_b[l].reshape(1, -1),
            "pwc_b": pwc_b[l].reshape(1, -1),
            "b_out": b_out[l].reshape(1, -1),
        }

    tm_lat = min(L, 512)
    tm_ctx = min(TC, 512)
    tm_in = min(B * L, 1024)
    x3 = x
    xc3 = ctx.reshape(1, TC, d)
    (sh, sc, _), (shc, scc, _) = mods(0)
    h = _norm(x3, norm_g[0].reshape(1, d), sc, sh, tm_lat)
    hc = _norm(xc3, norm_g[0].reshape(1, d), scc, shc, tm_ctx)
    zeros_h0 = jnp.zeros((B, 2, BRANCH_W), F32)

    for l in range(depth):
        last = l == depth - 1
        lw = layer_params(l)
        (_, _, gate), (_, _, gate_c) = mods(l)
        acc_c, hfin = _mix(hc.reshape(TC, d), wts, lw, l, B, CL, CL, zeros_h0, last, tm_ctx, CL, tm_ctx)
        acc, _ = _mix(h.reshape(B * L, d), wts, lw, l, B, L, GRID_W, hfin, False, tm_in, tm_lat,
                      min(B * L, 1024))
        if last:
            return _out_proj(acc.reshape(B, L, d), x3, wts["w_out"], l, lw["b_out"], gate,
                             final_g.reshape(1, d), None, None, tm_lat, True)
        (sh, sc, _), (shc, scc, _) = mods(l + 1)
        g_next = norm_g[l + 1].reshape(1, d)
        x3, h = _out_proj(acc.reshape(B, L, d), x3, wts["w_out"], l, lw["b_out"], gate, g_next, sc, sh,
                          tm_lat, False)
        xc3, hc = _out_proj(acc_c.reshape(1, TC, d), xc3, wts["w_out"], l, lw["b_out"], gate_c, g_next, scc,
                            shc, tm_ctx, False)
```

```python
import functools

import jax
import jax.numpy as jnp
import numpy as np
from jax import lax
from jax.experimental import pallas as pl
from jax.experimental.pallas import tpu as pltpu

F32 = jnp.float32
BF16 = jnp.bfloat16

D_MODEL = 2048
BRANCH_W = 1024
LRU_HEADS = 8
HEAD_DIM = BRANCH_W // LRU_HEADS
LRU_C = 8.0
CONV_A_WIDTH = 4
CONV_A_PAD_LEFT = 2
GRID_W = 64
POOL_WINDOWS = (2, 4, 8, 16)
POOL_GROUP = BRANCH_W // len(POOL_WINDOWS)
CONV_C_WIDTH = 31
CONV_C_PAD_LEFT = (CONV_C_WIDTH - 1) // 2
N_BRANCH = 3
EPS = 1e-6
BLK_XA, BLK_ZA, BLK_XB, BLK_ZB, BLK_CV, BLK_CG, BLK_ZC, BLK_GL = range(8)
N_ACT_BLOCKS = 3 + N_BRANCH * D_MODEL // BRANCH_W

SUBLANES = 8
V7X_VMEM_LIMIT_BYTES = 56 * 1024 * 1024


def _params(*sem):
    return pltpu.CompilerParams(dimension_semantics=sem, vmem_limit_bytes=V7X_VMEM_LIMIT_BYTES)


def _silu(v):
    return v * jax.nn.sigmoid(v)


def _norm_mod(x, g, scale, shift):
    y = x * lax.rsqrt(jnp.mean(x * x, axis=-1, keepdims=True) + EPS) * g
    return y * (1.0 + scale) + shift


def _ada_kernel(c_ref, w_ref, b_ref, o_ref):
    s = _silu(c_ref[...])
    o_ref[...] = jnp.dot(s.astype(BF16), w_ref[...].astype(BF16), preferred_element_type=F32) + b_ref[...]


def _ada(cc, w_ada, b_ada, tn=512):
    depth, d, n = w_ada.shape
    rows = cc.shape[0]
    return pl.pallas_call(
        _ada_kernel,
        out_shape=jax.ShapeDtypeStruct((depth, rows, n), F32),
        grid=(depth, n // tn),
        in_specs=[
            pl.BlockSpec((rows, d), lambda l, j: (0, 0)),
            pl.BlockSpec((None, d, tn), lambda l, j: (l, 0, j)),
            pl.BlockSpec((None, 1, tn), lambda l, j: (l, 0, j)),
        ],
        out_specs=pl.BlockSpec((None, rows, tn), lambda l, j: (l, 0, j)),
        compiler_params=_params("parallel", "parallel"),
        name="ada_mod",
    )(cc, w_ada, b_ada.reshape(depth, 1, n))


def _norm_kernel(x_ref, g_ref, sc_ref, sh_ref, h_ref):
    h_ref[...] = _norm_mod(x_ref[...], g_ref[...], sc_ref[...], sh_ref[...]).astype(BF16)


def _norm(x3, g, scale, shift, tm):
    G, T, d = x3.shape
    mod_spec = pl.BlockSpec((None, 1, d), lambda gi, i: (gi, 0, 0))
    return pl.pallas_call(
        _norm_kernel,
        out_shape=jax.ShapeDtypeStruct((G, T, d), BF16),
        grid=(G, T // tm),
        in_specs=[
            pl.BlockSpec((None, tm, d), lambda gi, i: (gi, i, 0)),
            pl.BlockSpec((1, d), lambda gi, i: (0, 0)),
            mod_spec, mod_spec,
        ],
        out_specs=pl.BlockSpec((None, tm, d), lambda gi, i: (gi, i, 0)),
        compiler_params=_params("parallel", "parallel"),
        name="norm_mod",
    )(x3, g, scale, shift)


def _cast_weights_once(w_ref, wbf_s):
    @pl.when(pl.program_id(1) == 0)
    def _():
        wbf_s[...] = w_ref[...].astype(BF16)


def _lin_kernel(h_ref, w_ref, o_ref, wbf_s):
    _cast_weights_once(w_ref, wbf_s)
    o_ref[...] = jnp.dot(h_ref[...], wbf_s[...], preferred_element_type=F32)


def _act_kernel(h_ref, w_ref, o_ref, wbf_s):
    _cast_weights_once(w_ref, wbf_s)
    acc = jnp.dot(h_ref[...], wbf_s[...], preferred_element_type=F32)
    s = jax.nn.sigmoid(acc)
    o_ref[...] = jnp.where(pl.program_id(0) < 3, acc * s, s).astype(BF16)


def _in_proj(kind, h, w_in, layer, n_blocks, tm):
    T, d = h.shape
    if kind == "lin":
        body, dtype = _lin_kernel, F32
        blk = lambda j: (BLK_XB - BLK_XA) * j
    else:
        body, dtype = _act_kernel, BF16
        blk = lambda j: jnp.where(j < 3, 1 + 2 * j + j // 2, j + BLK_GL - 3)
    return pl.pallas_call(
        body,
        out_shape=jax.ShapeDtypeStruct((T, n_blocks * BRANCH_W), dtype),
        grid=(n_blocks, T // tm),
        in_specs=[
            pl.BlockSpec((tm, d), lambda j, i: (i, 0)),
            pl.BlockSpec((None, d, BRANCH_W), lambda j, i: (layer, 0, blk(j))),
        ],
        out_specs=pl.BlockSpec((tm, BRANCH_W), lambda j, i: (i, j)),
        scratch_shapes=[pltpu.VMEM((d, BRANCH_W), BF16)],
        compiler_params=_params("parallel", "arbitrary"),
        name="in_proj_" + kind,
    )(h, w_in)


def _rglru_kernel(xa_ref, cw_ref, cb_ref, wri_ref, bri_ref, lam_ref, h0_ref, perm_ref,
                  ya_ref, hfin_ref, ua_s, hf_s, af_s, hb_s, ab_s, *, L, JB):
    S = L // SUBLANES
    xa = xa_ref[...]
    row = lax.broadcasted_iota(jnp.int32, (L, HEAD_DIM), 0)
    ua = jnp.zeros((L, HEAD_DIM), F32) + cb_ref[...]
    for k in range(CONV_A_WIDTH):
        o = k - CONV_A_PAD_LEFT
        sh = xa if o == 0 else pltpu.roll(xa, (-o) % L, axis=0)
        valid = (row + o >= 0) & (row + o < L)
        ua = ua + cw_ref[k:k + 1, :] * jnp.where(valid, sh, 0.0)
    for s in range(SUBLANES):
        for jb in range(S // SUBLANES):
            t0 = s * S + jb * SUBLANES
            ua_s[pl.ds(jb * SUBLANES * SUBLANES + s, SUBLANES, stride=SUBLANES), :] = ua[t0:t0 + SUBLANES]

    nlam = -lam_ref[...]
    sp = jnp.maximum(nlam, 0.0) + jnp.log1p(jnp.exp(-jnp.abs(nlam)))
    rate = -LRU_C * sp

    def coeffs(j0, d):
        u = ua_s[pl.ds(pl.multiple_of(j0 * SUBLANES, JB * SUBLANES), JB * SUBLANES), :]
        c0 = 2 * d * HEAD_DIM
        g = (jnp.dot(u.astype(BF16), wri_ref[:, c0:c0 + 2 * HEAD_DIM], preferred_element_type=F32)
             + bri_ref[:, c0:c0 + 2 * HEAD_DIM])
        r = jax.nn.sigmoid(g[:, :HEAD_DIM])
        ig = jax.nn.sigmoid(g[:, HEAD_DIM:])
        log_a = r * rate[d:d + 1, :]
        a = jnp.exp(log_a)
        mult = jnp.sqrt(-jnp.tanh(log_a) * (a * a + 1.0))
        return a, mult * ig * u

    def block(i, carry):
        hf, af, hb, ab = carry
        j0 = i * JB
        a, b = coeffs(j0, 0)
        for v in range(JB):
            av = a[v * SUBLANES:(v + 1) * SUBLANES]
            hf = av * hf + b[v * SUBLANES:(v + 1) * SUBLANES]
            af = av * af
            hf_s[j0 + v] = hf
            af_s[j0 + v] = af
        j0 = S - (i + 1) * JB
        a, b = coeffs(j0, 1)
        for v in reversed(range(JB)):
            av = a[v * SUBLANES:(v + 1) * SUBLANES]
            hb = av * hb + b[v * SUBLANES:(v + 1) * SUBLANES]
            ab = av * ab
            hb_s[j0 + v] = hb
            ab_s[j0 + v] = ab
        return hf, af, hb, ab

    zero = jnp.zeros((SUBLANES, HEAD_DIM), F32)
    one = jnp.ones((SUBLANES, HEAD_DIM), F32)
    hf, af, hb, ab = lax.fori_loop(0, S // JB, block, (zero, one, zero, one))

    cf = [h0_ref[0:1, :]]
    for s in range(SUBLANES):
        cf.append(hf[s:s + 1] + af[s:s + 1] * cf[-1])
    cb = [h0_ref[1:2, :]]
    for s in reversed(range(SUBLANES)):
        cb.append(hb[s:s + 1] + ab[s:s + 1] * cb[-1])
    cb = cb[::-1]
    hfin_ref[0:1, :] = cf[SUBLANES]
    hfin_ref[1:2, :] = cb[0]
    enter_f = jnp.concatenate(cf[:SUBLANES], axis=0)
    enter_b = jnp.concatenate(cb[1:], axis=0)

    JT = perm_ref.shape[0] // SUBLANES
    for blk in range(S // JT):
        steps = slice(blk * JT, (blk + 1) * JT)
        ysm = (hf_s[steps] + af_s[steps] * enter_f) + (hb_s[steps] + ab_s[steps] * enter_b)
        ynat = jnp.dot(perm_ref[...], ysm.reshape(JT * SUBLANES, HEAD_DIM).astype(BF16),
                       preferred_element_type=F32)
        for s in range(SUBLANES):
            ya_ref[s * S + blk * JT:s * S + (blk + 1) * JT, :] = ynat[s * JT:(s + 1) * JT].astype(BF16)


def _rglru(lin3, conv_w, conv_b, wri, bri, lam, h0):
    B, L, _ = lin3.shape
    S = L // SUBLANES
    JB = min(S, 256)
    JT = min(S, 32)
    perm = np.zeros((SUBLANES * JT, SUBLANES * JT), np.float32)
    for s in range(SUBLANES):
        for jj in range(JT):
            perm[s * JT + jj, jj * SUBLANES + s] = 1.0
    perm = jnp.asarray(perm, BF16)
    col = pl.BlockSpec((None, L, HEAD_DIM), lambda b, h: (b, 0, h))
    scr = pltpu.VMEM((S, SUBLANES, HEAD_DIM), F32)
    return pl.pallas_call(
        functools.partial(_rglru_kernel, L=L, JB=JB),
        out_shape=(jax.ShapeDtypeStruct((B, L, BRANCH_W), BF16),
                   jax.ShapeDtypeStruct((B, 2, BRANCH_W), F32)),
        grid=(B, LRU_HEADS),
        in_specs=[
            col,
            pl.BlockSpec((CONV_A_WIDTH, HEAD_DIM), lambda b, h: (0, h)),
            pl.BlockSpec((1, HEAD_DIM), lambda b, h: (0, h)),
            pl.BlockSpec((None, HEAD_DIM, 4 * HEAD_DIM), lambda b, h: (h, 0, 0)),
            pl.BlockSpec((None, 1, 4 * HEAD_DIM), lambda b, h: (h, 0, 0)),
            pl.BlockSpec((2, HEAD_DIM), lambda b, h: (0, h)),
            pl.BlockSpec((None, 2, HEAD_DIM), lambda b, h: (b, 0, h)),
            pl.BlockSpec(perm.shape, lambda b, h: (0, 0)),
        ],
        out_specs=(col, pl.BlockSpec((None, 2, HEAD_DIM), lambda b, h: (b, 0, h))),
        scratch_shapes=[pltpu.VMEM((L, HEAD_DIM), F32)] + [scr] * 4,
        compiler_params=_params("parallel", "parallel"),
        name="rglru",
    )(lin3, conv_w, conv_b, wri, bri, lam, h0, perm)


_CONVC_HALO = 16
GLU_CHUNK = 256
CONV_RB = 32


def _glu_conv_kernel(tap_ref, h_ref, wv_ref, wg_ref, cw_ref, cb_ref, o_ref, buf_s, *, tm, nt):
    H = _CONVC_HALO
    i = pl.program_id(1)

    @pl.when(i == 0)
    def _():
        buf_s[...] = jnp.zeros(buf_s.shape, F32)

    h = h_ref[...]
    rb = min(CONV_RB, tm)
    first = H - CONV_C_PAD_LEFT
    base = pl.multiple_of(tap_ref[0], HEAD_DIM)
    for c0 in range(0, BRANCH_W, GLU_CHUNK):
        cols = slice(c0, c0 + GLU_CHUNK)
        v = jnp.dot(h, wv_ref[:, cols], preferred_element_type=F32)
        g = jnp.dot(h, wg_ref[:, cols], preferred_element_type=F32)
        y = v * jax.nn.sigmoid(g)
        for l0 in range(c0, c0 + GLU_CHUNK, HEAD_DIM):
            lb = l0 // HEAD_DIM
            lanes = slice(l0, l0 + HEAD_DIM)
            yl = y[:, l0 - c0:l0 - c0 + HEAD_DIM]
            buf_s[lb, H + tm:2 * H + tm, :] = jnp.where(i < nt, yl[0:H], 0.0)
            for r0 in range(0, tm, rb):
                acc = jnp.zeros((rb, HEAD_DIM), F32) + cb_ref[:, lanes]
                for k in range(CONV_C_WIDTH):
                    acc = acc + cw_ref[k:k + 1, lanes] * buf_s[lb, pl.ds(base + (r0 + first + k), rb), :]
                o_ref[r0:r0 + rb, lanes] = acc
            buf_s[lb, 0:H, :] = buf_s[lb, tm:tm + H, :]
            buf_s[lb, H:H + tm, :] = yl


def _cast_kernel(w_ref, o_ref):
    o_ref[...] = w_ref[...].astype(BF16)


def _glu_weights(w_in, tk=512):
    depth, d, _ = w_in.shape
    return pl.pallas_call(
        _cast_kernel,
        out_shape=jax.ShapeDtypeStruct((depth, 2, d, BRANCH_W), BF16),
        grid=(depth, 2, d // tk),
        in_specs=[pl.BlockSpec((None, tk, BRANCH_W), lambda l, c, k: (l, k, BLK_CV + c))],
        out_specs=pl.BlockSpec((None, None, tk, BRANCH_W), lambda l, c, k: (l, c, k, 0)),
        compiler_params=_params("parallel", "parallel", "parallel"),
        name="glu_weights",
    )(w_in)


def _glu_conv(h3, w_glu, layer, conv_w, conv_b, tm):
    B, L, d = h3.shape
    nt = L // tm
    w_spec = lambda c: pl.BlockSpec((None, None, d, BRANCH_W), lambda b, i, t: (layer, c, 0, 0))
    taps = jnp.zeros((1,), jnp.int32)
    return pl.pallas_call(
        functools.partial(_glu_conv_kernel, tm=tm, nt=nt),
        out_shape=jax.ShapeDtypeStruct((B, L, BRANCH_W), F32),
        grid_spec=pltpu.PrefetchScalarGridSpec(
            num_scalar_prefetch=1,
            grid=(B, nt + 1),
            in_specs=[
                pl.BlockSpec((None, tm, d), lambda b, i, t: (b, jnp.minimum(i, nt - 1), 0)),
                w_spec(0), w_spec(1),
                pl.BlockSpec((CONV_C_WIDTH, BRANCH_W), lambda b, i, t: (0, 0)),
                pl.BlockSpec((1, BRANCH_W), lambda b, i, t: (0, 0)),
            ],
            out_specs=pl.BlockSpec((None, tm, BRANCH_W), lambda b, i, t: (b, jnp.maximum(i - 1, 0), 0)),
            scratch_shapes=[pltpu.VMEM((BRANCH_W // HEAD_DIM, tm + 2 * _CONVC_HALO, HEAD_DIM), F32)],
        ),
        compiler_params=_params("parallel", "arbitrary"),
        name="glu_conv",
    )(taps, h3, w_glu, w_glu, conv_w, conv_b)


def _pool_tail_kernel(xb_ref, szb_ref, yc_ref, szc_ref, pw_ref, ps_ref, lng_ref, lnb_ref, pwc_ref, pwcb_ref,
                      gb_ref, gc_ref, *, tm, n_row):
    pos = lax.broadcasted_iota(jnp.int32, (tm, POOL_GROUP), 0) % n_row
    for k, w in enumerate(POOL_WINDOWS):
        c0 = k * POOL_GROUP
        u = xb_ref[:, c0:c0 + POOL_GROUP]
        tot = u
        for o in range(-(w // 2), w // 2):
            if o == 0:
                continue
            sh = pltpu.roll(u, (-o) % tm, axis=0)
            tot = tot + jnp.where((pos + o >= 0) & (pos + o < n_row), sh, 0.0)
        cnt = jnp.minimum(pos + w // 2, n_row) - jnp.maximum(pos - w // 2, 0)
        p = tot / cnt.astype(F32) - u
        y = jnp.dot(p.astype(BF16), pw_ref[k], preferred_element_type=F32) * ps_ref[:, c0:c0 + POOL_GROUP]
        gb_ref[:, c0:c0 + POOL_GROUP] = (y * szb_ref[:, c0:c0 + POOL_GROUP].astype(F32)).astype(BF16)

    yc = yc_ref[...]
    mu = jnp.mean(yc, axis=-1, keepdims=True)
    dev = yc - mu
    var = jnp.mean(dev * dev, axis=-1, keepdims=True)
    t = _silu(dev * lax.rsqrt(var + EPS) * lng_ref[...] + lnb_ref[...])
    y = jnp.dot(t.astype(BF16), pwc_ref[...], preferred_element_type=F32) + pwcb_ref[...]
    gc_ref[...] = (y * szc_ref[...].astype(F32)).astype(BF16)


def _pool_tail(lin, act, yconv, layer, pool_w, pool_scale, ln_g, ln_b, pwc_w, pwc_b, tm, n_row):
    T = lin.shape[0]
    col = lambda blk: pl.BlockSpec((tm, BRANCH_W), lambda i: (i, blk))
    vec = pl.BlockSpec((1, BRANCH_W), lambda i: (0, 0))
    return pl.pallas_call(
        functools.partial(_pool_tail_kernel, tm=tm, n_row=n_row),
        out_shape=(jax.ShapeDtypeStruct((T, BRANCH_W), BF16), jax.ShapeDtypeStruct((T, BRANCH_W), BF16)),
        grid=(T // tm,),
        in_specs=[
            col(1), col(1), col(0), col(2),
            pl.BlockSpec((None,) + pool_w.shape[1:], lambda i: (layer, 0, 0, 0)),
            vec, vec, vec,
            pl.BlockSpec((None, BRANCH_W, BRANCH_W), lambda i: (layer, 0, 0)),
            vec,
        ],
        out_specs=(col(0), col(0)),
        compiler_params=_params("parallel"),
        name="pool_tail",
    )(lin, act, yconv, act, pool_w, pool_scale, ln_g, ln_b, pwc_w, pwc_b)


def _merge_kernel(ya_ref, sza_ref, gb_ref, gc_ref, w_ref, gt0_ref, gt1_ref, gt2_ref, o_ref):
    ga = ya_ref[...] * sza_ref[...]
    acc = None
    for k, (g, gt_ref) in enumerate(((ga, gt0_ref), (gb_ref[...], gt1_ref), (gc_ref[...], gt2_ref))):
        t = gt_ref[...].astype(F32) * jnp.dot(g, w_ref[k], preferred_element_type=F32)
        acc = t if acc is None else acc + t
    o_ref[...] = acc.astype(BF16)


def _merge(ya, gb, gc, w_bout, layer, act, tm, tn):
    T = gb.shape[0]
    g_spec = pl.BlockSpec((tm, BRANCH_W), lambda i, j: (i, 0))
    gate = lambda k: pl.BlockSpec((tm, tn), lambda i, j, o=(3 * BRANCH_W + k * D_MODEL) // tn: (i, o + j))
    return pl.pallas_call(
        _merge_kernel,
        out_shape=jax.ShapeDtypeStruct((T, D_MODEL), BF16),
        grid=(T // tm, D_MODEL // tn),
        in_specs=[
            g_spec, g_spec, g_spec, g_spec,
            pl.BlockSpec((None, N_BRANCH, BRANCH_W, tn), lambda i, j: (layer, 0, 0, j)),
            gate(0), gate(1), gate(2),
        ],
        out_specs=pl.BlockSpec((tm, tn), lambda i, j: (i, j)),
        compiler_params=_params("parallel", "parallel"),
        name="merge",
    )(ya, act, gb, gc, w_bout, act, act, act)


def _out_mid_kernel(acc_ref, x_ref, w_ref, b_ref, gate_ref, g_ref, sc_ref, sh_ref, xo_ref, h_ref):
    out = jnp.dot(acc_ref[...], w_ref[...], preferred_element_type=F32) + b_ref[...]
    xn = x_ref[...] + gate_ref[...] * out
    xo_ref[...] = xn
    h_ref[...] = _norm_mod(xn, g_ref[...], sc_ref[...], sh_ref[...]).astype(BF16)


def _out_last_kernel(acc_ref, x_ref, w_ref, b_ref, gate_ref, g_ref, o_ref):
    out = jnp.dot(acc_ref[...], w_ref[...], preferred_element_type=F32) + b_ref[...]
    xn = x_ref[...] + gate_ref[...] * out
    o_ref[...] = xn * lax.rsqrt(jnp.mean(xn * xn, axis=-1, keepdims=True) + EPS) * g_ref[...]


def _out_proj(acc3, x3, w_out, layer, b_out, gate, g_next, scale_next, shift_next, tm, last):
    G, T, d = x3.shape
    tile = pl.BlockSpec((None, tm, d), lambda gi, i: (gi, i, 0))
    mod = pl.BlockSpec((None, 1, d), lambda gi, i: (gi, 0, 0))
    vec = pl.BlockSpec((1, d), lambda gi, i: (0, 0))
    wspec = pl.BlockSpec((None, d, d), lambda gi, i: (layer, 0, 0))
    if last:
        return pl.pallas_call(
            _out_last_kernel,
            out_shape=jax.ShapeDtypeStruct((G, T, d), F32),
            grid=(G, T // tm),
            in_specs=[tile, tile, wspec, vec, mod, vec],
            out_specs=tile,
            compiler_params=_params("parallel", "parallel"),
            name="out_last",
        )(acc3, x3, w_out, b_out, gate, g_next)
    return pl.pallas_call(
        _out_mid_kernel,
        out_shape=(jax.ShapeDtypeStruct((G, T, d), F32), jax.ShapeDtypeStruct((G, T, d), BF16)),
        grid=(G, T // tm),
        in_specs=[tile, tile, wspec, vec, mod, vec, mod, mod],
        out_specs=(tile, tile),
        compiler_params=_params("parallel", "parallel"),
        name="out_mid",
    )(acc3, x3, w_out, b_out, gate, g_next, scale_next, shift_next)


ROWS_FULL_WIDTH = 512
ROWS_MATMUL = 1024
COLS_MERGE = 512
ROWS_GLU_CONV = 512


def _mix(h, wts, lw, layer, B, L, n_row, h0, state_only, tm_tail):
    T = B * L
    tm_in = min(T, ROWS_MATMUL)
    assert tm_tail % n_row == 0 and T % tm_in == 0
    lin = _in_proj("lin", h, wts["w_in"], layer, 1 if state_only else 2, tm_in)
    ya, hfin = _rglru(lin.reshape(B, L, -1), lw["conv_a_w"], lw["conv_a_b"], wts["wri"][layer],
                      wts["bri"][layer], lw["lam"], h0)
    if state_only:
        return None, hfin
    act = _in_proj("act", h, wts["w_in"], layer, N_ACT_BLOCKS, tm_in)
    yconv = _glu_conv(h.reshape(B, L, -1), wts["w_glu"], layer, lw["convc_w"], lw["convc_b"],
                      min(L, ROWS_GLU_CONV))
    gb, gc = _pool_tail(lin, act, yconv.reshape(T, BRANCH_W), layer, wts["pool_w"], lw["pool_scale"],
                        lw["lnc_g"], lw["lnc_b"], wts["pwc_w"], lw["pwc_b"], tm_tail, n_row)
    acc = _merge(ya.reshape(T, BRANCH_W), gb, gc, wts["w_bout"], layer, act, tm_in, COLS_MERGE)
    return acc, hfin


def kernel(x, c, ctx, c_ctx, norm_g, w_ada, b_ada, w_in, conv_a_w, conv_a_b, lru_wr, lru_br, lru_wi, lru_bi,
           lru_lambda, pool_w, pool_scale, convc_w, convc_b, lnc_g, lnc_b, pwc_w, pwc_b, w_bout, w_out, b_out,
           final_g):
    B, L, d = x.shape
    CL = ctx.shape[1]
    depth = w_in.shape[0]
    TC = B * CL

    rows = -(-(B + 1) // SUBLANES) * SUBLANES
    cc = jnp.zeros((rows, d), F32).at[:B].set(c).at[B].set(c_ctx)
    mod = _ada(cc, w_ada, b_ada)

    def mods(l):
        sh, sc, gt = (mod[l, :, k * d:(k + 1) * d] for k in range(3))
        lat = tuple(v[:B].reshape(B, 1, d) for v in (sh, sc, gt))
        cx = tuple(v[B:B + 1].reshape(1, 1, d) for v in (sh, sc, gt))
        return lat, cx

    wts = {
        "w_in": w_in,
        "w_glu": _glu_weights(w_in),
        "wri": jnp.concatenate([lru_wr[:, 0], lru_wi[:, 0], lru_wr[:, 1], lru_wi[:, 1]], axis=-1).astype(BF16),
        "bri": jnp.concatenate([lru_br[:, 0], lru_bi[:, 0], lru_br[:, 1], lru_bi[:, 1]],
                               axis=-1).reshape(depth, LRU_HEADS, 1, 4 * HEAD_DIM),
        "pool_w": pool_w.astype(BF16),
        "pwc_w": pwc_w.astype(BF16),
        "w_bout": w_bout.astype(BF16),
        "w_out": w_out.astype(BF16),
    }

    def layer_params(l):
        return {
            "conv_a_w": conv_a_w[l], "conv_a_b": conv_a_b[l].reshape(1, -1),
            "lam": lru_lambda[l],
            "pool_scale": pool_scale[l].reshape(1, -1),
            "convc_w": convc_w[l], "convc_b": convc_b[l].reshape(1, -1),
            "lnc_g": lnc_g[l].reshape(1, -1), "lnc_b": lnc_b[l].reshape(1, -1),
            "pwc_b": pwc_b[l].reshape(1, -1),
            "b_out": b_out[l].reshape(1, -1),
        }

    tm_lat = min(L, ROWS_FULL_WIDTH)
    tm_ctx = min(TC, ROWS_FULL_WIDTH)
    x3 = x
    xc3 = ctx.reshape(1, TC, d)
    (sh, sc, _), (shc, scc, _) = mods(0)
    h = _norm(x3, norm_g[0].reshape(1, d), sc, sh, tm_lat)
    hc = _norm(xc3, norm_g[0].reshape(1, d), scc, shc, tm_ctx)
    zeros_h0 = jnp.zeros((B, 2, BRANCH_W), F32)

    for l in range(depth):
        last = l == depth - 1
        lw = layer_params(l)
        (_, _, gate), (_, _, gate_c) = mods(l)
        acc_c, hfin = _mix(hc.reshape(TC, d), wts, lw, l, B, CL, CL, zeros_h0, last, tm_ctx)
        acc, _ = _mix(h.reshape(B * L, d), wts, lw, l, B, L, GRID_W, hfin, False, tm_lat)
        if last:
            return _out_proj(acc.reshape(B, L, d), x3, wts["w_out"], l, lw["b_out"], gate,
                             final_g.reshape(1, d), None, None, tm_lat, True)
        (sh, sc, _), (shc, scc, _) = mods(l + 1)
        g_next = norm_g[l + 1].reshape(1, d)
        x3, h = _out_proj(acc.reshape(B, L, d), x3, wts["w_out"], l, lw["b_out"], gate, g_next, sc, sh,
                          tm_lat, False)
        xc3, hc = _out_proj(acc_c.reshape(1, TC, d), xc3, wts["w_out"], l, lw["b_out"], gate_c, g_next, scc,
                            shc, tm_ctx, False)
```

```python
import functools

import jax
import jax.numpy as jnp
import numpy as np
from jax import lax
from jax.experimental import pallas as pl
from jax.experimental.pallas import tpu as pltpu

F32 = jnp.float32
BF16 = jnp.bfloat16

D_MODEL = 2048
BRANCH_W = 1024
LRU_HEADS = 8
HEAD_DIM = BRANCH_W // LRU_HEADS
LRU_C = 8.0
CONV_A_WIDTH = 4
CONV_A_PAD_LEFT = 2
GRID_W = 64
POOL_WINDOWS = (2, 4, 8, 16)
POOL_GROUP = BRANCH_W // len(POOL_WINDOWS)
CONV_C_WIDTH = 31
CONV_C_PAD_LEFT = (CONV_C_WIDTH - 1) // 2
N_BRANCH = 3
EPS = 1e-6
BLK_XA, BLK_ZA, BLK_XB, BLK_ZB, BLK_CV, BLK_CG, BLK_ZC, BLK_GL = range(8)
N_ACT_BLOCKS = 3 + N_BRANCH * D_MODEL // BRANCH_W

SUBLANES = 8
V7X_VMEM_LIMIT_BYTES = 56 * 1024 * 1024


def _params(*sem):
    return pltpu.CompilerParams(dimension_semantics=sem, vmem_limit_bytes=V7X_VMEM_LIMIT_BYTES)


def _silu(v):
    return v * jax.nn.sigmoid(v)


def _norm_mod(x, g, scale, shift):
    y = x * lax.rsqrt(jnp.mean(x * x, axis=-1, keepdims=True) + EPS) * g
    return y * (1.0 + scale) + shift


def _ada_kernel(c_ref, w_ref, b_ref, o_ref):
    s = _silu(c_ref[...])
    o_ref[...] = jnp.dot(s.astype(BF16), w_ref[...].astype(BF16), preferred_element_type=F32) + b_ref[...]


def _ada(cc, w_ada, b_ada, tn=512):
    depth, d, n = w_ada.shape
    rows = cc.shape[0]
    return pl.pallas_call(
        _ada_kernel,
        out_shape=jax.ShapeDtypeStruct((depth, rows, n), F32),
        grid=(depth, n // tn),
        in_specs=[
            pl.BlockSpec((rows, d), lambda l, j: (0, 0)),
            pl.BlockSpec((None, d, tn), lambda l, j: (l, 0, j)),
            pl.BlockSpec((None, 1, tn), lambda l, j: (l, 0, j)),
        ],
        out_specs=pl.BlockSpec((None, rows, tn), lambda l, j: (l, 0, j)),
        compiler_params=_params("parallel", "parallel"),
        name="ada_mod",
    )(cc, w_ada, b_ada.reshape(depth, 1, n))


def _norm_kernel(x_ref, g_ref, sc_ref, sh_ref, h_ref):
    h_ref[...] = _norm_mod(x_ref[...], g_ref[...], sc_ref[...], sh_ref[...]).astype(BF16)


def _norm(x3, g, scale, shift, tm):
    G, T, d = x3.shape
    mod_spec = pl.BlockSpec((None, 1, d), lambda gi, i: (gi, 0, 0))
    return pl.pallas_call(
        _norm_kernel,
        out_shape=jax.ShapeDtypeStruct((G, T, d), BF16),
        grid=(G, T // tm),
        in_specs=[
            pl.BlockSpec((None, tm, d), lambda gi, i: (gi, i, 0)),
            pl.BlockSpec((1, d), lambda gi, i: (0, 0)),
            mod_spec, mod_spec,
        ],
        out_specs=pl.BlockSpec((None, tm, d), lambda gi, i: (gi, i, 0)),
        compiler_params=_params("parallel", "parallel"),
        name="norm_mod",
    )(x3, g, scale, shift)


def _cast_weights_once(w_ref, wbf_s):
    @pl.when(pl.program_id(1) == 0)
    def _():
        wbf_s[...] = w_ref[...].astype(BF16)


def _lin_kernel(h_ref, w_ref, o_ref, wbf_s):
    _cast_weights_once(w_ref, wbf_s)
    o_ref[...] = jnp.dot(h_ref[...], wbf_s[...], preferred_element_type=F32)


def _act_kernel(h_ref, w_ref, o_ref, wbf_s):
    _cast_weights_once(w_ref, wbf_s)
    acc = jnp.dot(h_ref[...], wbf_s[...], preferred_element_type=F32)
    s = jax.nn.sigmoid(acc)
    o_ref[...] = jnp.where(pl.program_id(0) < 3, acc * s, s).astype(BF16)


def _in_proj(kind, h, w_in, layer, n_blocks, tm):
    T, d = h.shape
    if kind == "lin":
        body, dtype = _lin_kernel, F32
        blk = lambda j: (BLK_XB - BLK_XA) * j
    else:
        body, dtype = _act_kernel, BF16
        blk = lambda j: jnp.where(j < 3, 1 + 2 * j + j // 2, j + BLK_GL - 3)
    return pl.pallas_call(
        body,
        out_shape=jax.ShapeDtypeStruct((T, n_blocks * BRANCH_W), dtype),
        grid=(n_blocks, T // tm),
        in_specs=[
            pl.BlockSpec((tm, d), lambda j, i: (i, 0)),
            pl.BlockSpec((None, d, BRANCH_W), lambda j, i: (layer, 0, blk(j))),
        ],
        out_specs=pl.BlockSpec((tm, BRANCH_W), lambda j, i: (i, j)),
        scratch_shapes=[pltpu.VMEM((d, BRANCH_W), BF16)],
        compiler_params=_params("parallel", "arbitrary"),
        name="in_proj_" + kind,
    )(h, w_in)


def _rglru_kernel(xa_ref, cw_ref, cb_ref, wri_ref, bri_ref, lam_ref, h0_ref, perm_ref,
                  ya_ref, hfin_ref, ua_s, hf_s, af_s, hb_s, ab_s, *, L, JB):
    S = L // SUBLANES
    xa = xa_ref[...]
    row = lax.broadcasted_iota(jnp.int32, (L, HEAD_DIM), 0)
    ua = jnp.zeros((L, HEAD_DIM), F32) + cb_ref[...]
    for k in range(CONV_A_WIDTH):
        o = k - CONV_A_PAD_LEFT
        sh = xa if o == 0 else pltpu.roll(xa, (-o) % L, axis=0)
        valid = (row + o >= 0) & (row + o < L)
        ua = ua + cw_ref[k:k + 1, :] * jnp.where(valid, sh, 0.0)
    for s in range(SUBLANES):
        for jb in range(S // SUBLANES):
            t0 = s * S + jb * SUBLANES
            ua_s[pl.ds(jb * SUBLANES * SUBLANES + s, SUBLANES, stride=SUBLANES), :] = ua[t0:t0 + SUBLANES]

    nlam = -lam_ref[...]
    sp = jnp.maximum(nlam, 0.0) + jnp.log1p(jnp.exp(-jnp.abs(nlam)))
    rate = -LRU_C * sp

    def coeffs(j0, d):
        u = ua_s[pl.ds(pl.multiple_of(j0 * SUBLANES, JB * SUBLANES), JB * SUBLANES), :]
        c0 = 2 * d * HEAD_DIM
        g = (jnp.dot(u.astype(BF16), wri_ref[:, c0:c0 + 2 * HEAD_DIM], preferred_element_type=F32)
             + bri_ref[:, c0:c0 + 2 * HEAD_DIM])
        r = jax.nn.sigmoid(g[:, :HEAD_DIM])
        ig = jax.nn.sigmoid(g[:, HEAD_DIM:])
        log_a = r * rate[d:d + 1, :]
        a = jnp.exp(log_a)
        mult = jnp.sqrt(-jnp.tanh(log_a) * (a * a + 1.0))
        return a, mult * ig * u

    def block(i, carry):
        hf, af, hb, ab = carry
        j0 = i * JB
        a, b = coeffs(j0, 0)
        for v in range(JB):
            av = a[v * SUBLANES:(v + 1) * SUBLANES]
            hf = av * hf + b[v * SUBLANES:(v + 1) * SUBLANES]
            af = av * af
            hf_s[j0 + v] = hf
            af_s[j0 + v] = af
        j0 = S - (i + 1) * JB
        a, b = coeffs(j0, 1)
        for v in reversed(range(JB)):
            av = a[v * SUBLANES:(v + 1) * SUBLANES]
            hb = av * hb + b[v * SUBLANES:(v + 1) * SUBLANES]
            ab = av * ab
            hb_s[j0 + v] = hb
            ab_s[j0 + v] = ab
        return hf, af, hb, ab

    zero = jnp.zeros((SUBLANES, HEAD_DIM), F32)
    one = jnp.ones((SUBLANES, HEAD_DIM), F32)
    hf, af, hb, ab = lax.fori_loop(0, S // JB, block, (zero, one, zero, one))

    cf = [h0_ref[0:1, :]]
    for s in range(SUBLANES):
        cf.append(hf[s:s + 1] + af[s:s + 1] * cf[-1])
    cb = [h0_ref[1:2, :]]
    for s in reversed(range(SUBLANES)):
        cb.append(hb[s:s + 1] + ab[s:s + 1] * cb[-1])
    cb = cb[::-1]
    hfin_ref[0:1, :] = cf[SUBLANES]
    hfin_ref[1:2, :] = cb[0]
    enter_f = jnp.concatenate(cf[:SUBLANES], axis=0)
    enter_b = jnp.concatenate(cb[1:], axis=0)

    JT = perm_ref.shape[0] // SUBLANES
    for blk in range(S // JT):
        steps = slice(blk * JT, (blk + 1) * JT)
        ysm = (hf_s[steps] + af_s[steps] * enter_f) + (hb_s[steps] + ab_s[steps] * enter_b)
        ynat = jnp.dot(perm_ref[...], ysm.reshape(JT * SUBLANES, HEAD_DIM).astype(BF16),
                       preferred_element_type=F32)
        for s in range(SUBLANES):
            ya_ref[s * S + blk * JT:s * S + (blk + 1) * JT, :] = ynat[s * JT:(s + 1) * JT].astype(BF16)


def _rglru(lin3, conv_w, conv_b, wri, bri, lam, h0):
    B, L, _ = lin3.shape
    S = L // SUBLANES
    JB = min(S, 256)
    JT = min(S, 32)
    perm = np.zeros((SUBLANES * JT, SUBLANES * JT), np.float32)
    for s in range(SUBLANES):
        for jj in range(JT):
            perm[s * JT + jj, jj * SUBLANES + s] = 1.0
    perm = jnp.asarray(perm, BF16)
    col = pl.BlockSpec((None, L, HEAD_DIM), lambda b, h: (b, 0, h))
    scr = pltpu.VMEM((S, SUBLANES, HEAD_DIM), F32)
    return pl.pallas_call(
        functools.partial(_rglru_kernel, L=L, JB=JB),
        out_shape=(jax.ShapeDtypeStruct((B, L, BRANCH_W), BF16),
                   jax.ShapeDtypeStruct((B, 2, BRANCH_W), F32)),
        grid=(B, LRU_HEADS),
        in_specs=[
            col,
            pl.BlockSpec((CONV_A_WIDTH, HEAD_DIM), lambda b, h: (0, h)),
            pl.BlockSpec((1, HEAD_DIM), lambda b, h: (0, h)),
            pl.BlockSpec((None, HEAD_DIM, 4 * HEAD_DIM), lambda b, h: (h, 0, 0)),
            pl.BlockSpec((None, 1, 4 * HEAD_DIM), lambda b, h: (h, 0, 0)),
            pl.BlockSpec((2, HEAD_DIM), lambda b, h: (0, h)),
            pl.BlockSpec((None, 2, HEAD_DIM), lambda b, h: (b, 0, h)),
            pl.BlockSpec(perm.shape, lambda b, h: (0, 0)),
        ],
        out_specs=(col, pl.BlockSpec((None, 2, HEAD_DIM), lambda b, h: (b, 0, h))),
        scratch_shapes=[pltpu.VMEM((L, HEAD_DIM), F32)] + [scr] * 4,
        compiler_params=_params("parallel", "parallel"),
        name="rglru",
    )(lin3, conv_w, conv_b, wri, bri, lam, h0, perm)


_CONVC_HALO = 16
GLU_CHUNK = 256
CONV_RB = 32


def _glu_conv_kernel(tap_ref, h_ref, wv_ref, wg_ref, cw_ref, cb_ref, o_ref, buf_s, *, tm, nt):
    H = _CONVC_HALO
    i = pl.program_id(1)

    @pl.when(i == 0)
    def _():
        buf_s[...] = jnp.zeros(buf_s.shape, F32)

    h = h_ref[...]
    rb = min(CONV_RB, tm)
    first = H - CONV_C_PAD_LEFT
    base = pl.multiple_of(tap_ref[0], HEAD_DIM)
    for c0 in range(0, BRANCH_W, GLU_CHUNK):
        cols = slice(c0, c0 + GLU_CHUNK)
        v = jnp.dot(h, wv_ref[:, cols], preferred_element_type=F32)
        g = jnp.dot(h, wg_ref[:, cols], preferred_element_type=F32)
        y = v * jax.nn.sigmoid(g)
        for l0 in range(c0, c0 + GLU_CHUNK, HEAD_DIM):
            lb = l0 // HEAD_DIM
            lanes = slice(l0, l0 + HEAD_DIM)
            yl = y[:, l0 - c0:l0 - c0 + HEAD_DIM]
            buf_s[lb, H + tm:2 * H + tm, :] = jnp.where(i < nt, yl[0:H], 0.0)
            for r0 in range(0, tm, rb):
                acc = jnp.zeros((rb, HEAD_DIM), F32) + cb_ref[:, lanes]
                for k in range(CONV_C_WIDTH):
                    acc = acc + cw_ref[k:k + 1, lanes] * buf_s[lb, pl.ds(base + (r0 + first + k), rb), :]
                o_ref[r0:r0 + rb, lanes] = acc
            buf_s[lb, 0:H, :] = buf_s[lb, tm:tm + H, :]
            buf_s[lb, H:H + tm, :] = yl


def _cast_kernel(w_ref, o_ref):
    o_ref[...] = w_ref[...].astype(BF16)


def _glu_weights(w_in, tk=512):
    depth, d, _ = w_in.shape
    return pl.pallas_call(
        _cast_kernel,
        out_shape=jax.ShapeDtypeStruct((depth, 2, d, BRANCH_W), BF16),
        grid=(depth, 2, d // tk),
        in_specs=[pl.BlockSpec((None, tk, BRANCH_W), lambda l, c, k: (l, k, BLK_CV + c))],
        out_specs=pl.BlockSpec((None, None, tk, BRANCH_W), lambda l, c, k: (l, c, k, 0)),
        compiler_params=_params("parallel", "parallel", "parallel"),
        name="glu_weights",
    )(w_in)


def _glu_conv(h3, w_glu, layer, conv_w, conv_b, tm):
    B, L, d = h3.shape
    nt = L // tm
    w_spec = lambda c: pl.BlockSpec((None, None, d, BRANCH_W), lambda b, i, t: (layer, c, 0, 0))
    taps = jnp.zeros((1,), jnp.int32)
    return pl.pallas_call(
        functools.partial(_glu_conv_kernel, tm=tm, nt=nt),
        out_shape=jax.ShapeDtypeStruct((B, L, BRANCH_W), F32),
        grid_spec=pltpu.PrefetchScalarGridSpec(
            num_scalar_prefetch=1,
            grid=(B, nt + 1),
            in_specs=[
                pl.BlockSpec((None, tm, d), lambda b, i, t: (b, jnp.minimum(i, nt - 1), 0)),
                w_spec(0), w_spec(1),
                pl.BlockSpec((CONV_C_WIDTH, BRANCH_W), lambda b, i, t: (0, 0)),
                pl.BlockSpec((1, BRANCH_W), lambda b, i, t: (0, 0)),
            ],
            out_specs=pl.BlockSpec((None, tm, BRANCH_W), lambda b, i, t: (b, jnp.maximum(i - 1, 0), 0)),
            scratch_shapes=[pltpu.VMEM((BRANCH_W // HEAD_DIM, tm + 2 * _CONVC_HALO, HEAD_DIM), F32)],
        ),
        compiler_params=_params("parallel", "arbitrary"),
        name="glu_conv",
    )(taps, h3, w_glu, w_glu, conv_w, conv_b)


def _pool_tail_kernel(xb_ref, szb_ref, yc_ref, szc_ref, pw_ref, ps_ref, lng_ref, lnb_ref, pwc_ref, pwcb_ref,
                      gb_ref, gc_ref, *, tm, n_row):
    pos = lax.broadcasted_iota(jnp.int32, (tm, POOL_GROUP), 0) % n_row
    for k, w in enumerate(POOL_WINDOWS):
        c0 = k * POOL_GROUP
        u = xb_ref[:, c0:c0 + POOL_GROUP]
        tot = u
        for o in range(-(w // 2), w // 2):
            if o == 0:
                continue
            sh = pltpu.roll(u, (-o) % tm, axis=0)
            tot = tot + jnp.where((pos + o >= 0) & (pos + o < n_row), sh, 0.0)
        cnt = jnp.minimum(pos + w // 2, n_row) - jnp.maximum(pos - w // 2, 0)
        p = tot / cnt.astype(F32) - u
        y = jnp.dot(p.astype(BF16), pw_ref[k], preferred_element_type=F32) * ps_ref[:, c0:c0 + POOL_GROUP]
        gb_ref[:, c0:c0 + POOL_GROUP] = (y * szb_ref[:, c0:c0 + POOL_GROUP].astype(F32)).astype(BF16)

    yc = yc_ref[...]
    mu = jnp.mean(yc, axis=-1, keepdims=True)
    dev = yc - mu
    var = jnp.mean(dev * dev, axis=-1, keepdims=True)
    t = _silu(dev * lax.rsqrt(var + EPS) * lng_ref[...] + lnb_ref[...])
    y = jnp.dot(t.astype(BF16), pwc_ref[...], preferred_element_type=F32) + pwcb_ref[...]
    gc_ref[...] = (y * szc_ref[...].astype(F32)).astype(BF16)


def _pool_tail(lin, act, yconv, layer, pool_w, pool_scale, ln_g, ln_b, pwc_w, pwc_b, tm, n_row):
    T = lin.shape[0]
    col = lambda blk: pl.BlockSpec((tm, BRANCH_W), lambda i: (i, blk))
    vec = pl.BlockSpec((1, BRANCH_W), lambda i: (0, 0))
    return pl.pallas_call(
        functools.partial(_pool_tail_kernel, tm=tm, n_row=n_row),
        out_shape=(jax.ShapeDtypeStruct((T, BRANCH_W), BF16), jax.ShapeDtypeStruct((T, BRANCH_W), BF16)),
        grid=(T // tm,),
        in_specs=[
            col(1), col(1), col(0), col(2),
            pl.BlockSpec((None,) + pool_w.shape[1:], lambda i: (layer, 0, 0, 0)),
            vec, vec, vec,
            pl.BlockSpec((None, BRANCH_W, BRANCH_W), lambda i: (layer, 0, 0)),
            vec,
        ],
        out_specs=(col(0), col(0)),
        compiler_params=_params("parallel"),
        name="pool_tail",
    )(lin, act, yconv, act, pool_w, pool_scale, ln_g, ln_b, pwc_w, pwc_b)


def _merge_kernel(ya_ref, sza_ref, gb_ref, gc_ref, w_ref, gt0_ref, gt1_ref, gt2_ref, o_ref):
    ga = ya_ref[...] * sza_ref[...]
    acc = None
    for k, (g, gt_ref) in enumerate(((ga, gt0_ref), (gb_ref[...], gt1_ref), (gc_ref[...], gt2_ref))):
        t = gt_ref[...].astype(F32) * jnp.dot(g, w_ref[k], preferred_element_type=F32)
        acc = t if acc is None else acc + t
    o_ref[...] = acc.astype(BF16)


def _merge(ya, gb, gc, w_bout, layer, act, tm, tn):
    T = gb.shape[0]
    g_spec = pl.BlockSpec((tm, BRANCH_W), lambda i, j: (i, 0))
    gate = lambda k: pl.BlockSpec((tm, tn), lambda i, j, o=(3 * BRANCH_W + k * D_MODEL) // tn: (i, o + j))
    return pl.pallas_call(
        _merge_kernel,
        out_shape=jax.ShapeDtypeStruct((T, D_MODEL), BF16),
        grid=(T // tm, D_MODEL // tn),
        in_specs=[
            g_spec, g_spec, g_spec, g_spec,
            pl.BlockSpec((None, N_BRANCH, BRANCH_W, tn), lambda i, j: (layer, 0, 0, j)),
            gate(0), gate(1), gate(2),
        ],
        out_specs=pl.BlockSpec((tm, tn), lambda i, j: (i, j)),
        compiler_params=_params("parallel", "parallel"),
        name="merge",
    )(ya, act, gb, gc, w_bout, act, act, act)


def _out_mid_kernel(acc_ref, x_ref, w_ref, b_ref, gate_ref, g_ref, sc_ref, sh_ref, xo_ref, h_ref):
    out = jnp.dot(acc_ref[...], w_ref[...], preferred_element_type=F32) + b_ref[...]
    xn = x_ref[...] + gate_ref[...] * out
    xo_ref[...] = xn
    h_ref[...] = _norm_mod(xn, g_ref[...], sc_ref[...], sh_ref[...]).astype(BF16)


def _out_last_kernel(acc_ref, x_ref, w_ref, b_ref, gate_ref, g_ref, o_ref):
    out = jnp.dot(acc_ref[...], w_ref[...], preferred_element_type=F32) + b_ref[...]
    xn = x_ref[...] + gate_ref[...] * out
    o_ref[...] = xn * lax.rsqrt(jnp.mean(xn * xn, axis=-1, keepdims=True) + EPS) * g_ref[...]


def _out_proj(acc3, x3, w_out, layer, b_out, gate, g_next, scale_next, shift_next, tm, last):
    G, T, d = x3.shape
    tile = pl.BlockSpec((None, tm, d), lambda gi, i: (gi, i, 0))
    mod = pl.BlockSpec((None, 1, d), lambda gi, i: (gi, 0, 0))
    vec = pl.BlockSpec((1, d), lambda gi, i: (0, 0))
    wspec = pl.BlockSpec((None, d, d), lambda gi, i: (layer, 0, 0))
    if last:
        return pl.pallas_call(
            _out_last_kernel,
            out_shape=jax.ShapeDtypeStruct((G, T, d), F32),
            grid=(G, T // tm),
            in_specs=[tile, tile, wspec, vec, mod, vec],
            out_specs=tile,
            compiler_params=_params("parallel", "parallel"),
            name="out_last",
        )(acc3, x3, w_out, b_out, gate, g_next)
    return pl.pallas_call(
        _out_mid_kernel,
        out_shape=(jax.ShapeDtypeStruct((G, T, d), F32), jax.ShapeDtypeStruct((G, T, d), BF16)),
        grid=(G, T // tm),
        in_specs=[tile, tile, wspec, vec, mod, vec, mod, mod],
        out_specs=(tile, tile),
        compiler_params=_params("parallel", "parallel"),
        name="out_mid",
    )(acc3, x3, w_out, b_out, gate, g_next, scale_next, shift_next)


ROWS_FULL_WIDTH = 512
ROWS_MATMUL = 1024
COLS_MERGE = 1024
ROWS_GLU_CONV = 512


def _mix(h, wts, lw, layer, B, L, n_row, h0, state_only, tm_tail):
    T = B * L
    tm_in = min(T, ROWS_MATMUL)
    assert tm_tail % n_row == 0 and T % tm_in == 0
    lin = _in_proj("lin", h, wts["w_in"], layer, 1 if state_only else 2, tm_in)
    ya, hfin = _rglru(lin.reshape(B, L, -1), lw["conv_a_w"], lw["conv_a_b"], wts["wri"][layer],
                      wts["bri"][layer], lw["lam"], h0)
    if state_only:
        return None, hfin
    act = _in_proj("act", h, wts["w_in"], layer, N_ACT_BLOCKS, tm_in)
    yconv = _glu_conv(h.reshape(B, L, -1), wts["w_glu"], layer, lw["convc_w"], lw["convc_b"],
                      min(L, ROWS_GLU_CONV))
    gb, gc = _pool_tail(lin, act, yconv.reshape(T, BRANCH_W), layer, wts["pool_w"], lw["pool_scale"],
                        lw["lnc_g"], lw["lnc_b"], wts["pwc_w"], lw["pwc_b"], tm_tail, n_row)
    acc = _merge(ya.reshape(T, BRANCH_W), gb, gc, wts["w_bout"], layer, act, tm_in, COLS_MERGE)
    return acc, hfin


def kernel(x, c, ctx, c_ctx, norm_g, w_ada, b_ada, w_in, conv_a_w, conv_a_b, lru_wr, lru_br, lru_wi, lru_bi,
           lru_lambda, pool_w, pool_scale, convc_w, convc_b, lnc_g, lnc_b, pwc_w, pwc_b, w_bout, w_out, b_out,
           final_g):
    B, L, d = x.shape
    CL = ctx.shape[1]
    depth = w_in.shape[0]
    TC = B * CL

    rows = -(-(B + 1) // SUBLANES) * SUBLANES
    cc = jnp.zeros((rows, d), F32).at[:B].set(c).at[B].set(c_ctx)
    mod = _ada(cc, w_ada, b_ada)

    def mods(l):
        sh, sc, gt = (mod[l, :, k * d:(k + 1) * d] for k in range(3))
        lat = tuple(v[:B].reshape(B, 1, d) for v in (sh, sc, gt))
        cx = tuple(v[B:B + 1].reshape(1, 1, d) for v in (sh, sc, gt))
        return lat, cx

    wts = {
        "w_in": w_in,
        "w_glu": _glu_weights(w_in),
        "wri": jnp.concatenate([lru_wr[:, 0], lru_wi[:, 0], lru_wr[:, 1], lru_wi[:, 1]], axis=-1).astype(BF16),
        "bri": jnp.concatenate([lru_br[:, 0], lru_bi[:, 0], lru_br[:, 1], lru_bi[:, 1]],
                               axis=-1).reshape(depth, LRU_HEADS, 1, 4 * HEAD_DIM),
        "pool_w": pool_w.astype(BF16),
        "pwc_w": pwc_w.astype(BF16),
        "w_bout": w_bout.astype(BF16),
        "w_out": w_out.astype(BF16),
    }

    def layer_params(l):
        return {
            "conv_a_w": conv_a_w[l], "conv_a_b": conv_a_b[l].reshape(1, -1),
            "lam": lru_lambda[l],
            "pool_scale": pool_scale[l].reshape(1, -1),
            "convc_w": convc_w[l], "convc_b": convc_b[l].reshape(1, -1),
            "lnc_g": lnc_g[l].reshape(1, -1), "lnc_b": lnc_b[l].reshape(1, -1),
            "pwc_b": pwc_b[l].reshape(1, -1),
            "b_out": b_out[l].reshape(1, -1),
        }

    tm_lat = min(L, ROWS_FULL_WIDTH)
    tm_ctx = min(TC, ROWS_FULL_WIDTH)
    x3 = x
    xc3 = ctx.reshape(1, TC, d)
    (sh, sc, _), (shc, scc, _) = mods(0)
    h = _norm(x3, norm_g[0].reshape(1, d), sc, sh, tm_lat)
    hc = _norm(xc3, norm_g[0].reshape(1, d), scc, shc, tm_ctx)
    zeros_h0 = jnp.zeros((B, 2, BRANCH_W), F32)

    for l in range(depth):
        last = l == depth - 1
        lw = layer_params(l)
        (_, _, gate), (_, _, gate_c) = mods(l)
        acc_c, hfin = _mix(hc.reshape(TC, d), wts, lw, l, B, CL, CL, zeros_h0, last, tm_ctx)
        acc, _ = _mix(h.reshape(B * L, d), wts, lw, l, B, L, GRID_W, hfin, False, tm_lat)
        if last:
            return _out_proj(acc.reshape(B, L, d), x3, wts["w_out"], l, lw["b_out"], gate,
                             final_g.reshape(1, d), None, None, tm_lat, True)
        (sh, sc, _), (shc, scc, _) = mods(l + 1)
        g_next = norm_g[l + 1].reshape(1, d)
        x3, h = _out_proj(acc.reshape(B, L, d), x3, wts["w_out"], l, lw["b_out"], gate, g_next, sc, sh,
                          tm_lat, False)
        xc3, hc = _out_proj(acc_c.reshape(1, TC, d), xc3, wts["w_out"], l, lw["b_out"], gate_c, g_next, scc,
                            shc, tm_ctx, False)
```
